```python
import math
import jax, jax.numpy as jnp
from jax import lax
import numpy as np

D_MODEL = 1024
BATCH = 8
SEQ = 2048
DEPTH = 1
DEC_BATCH = 128
DEC_SEQ = 1
PAST_LEN = 16384
PAGE_SIZE = 128

SSM_WIDTH = D_MODEL // 2
SSM_GROUP = 16
SSM_GROUPS = SSM_WIDTH // SSM_GROUP
SSM_STATE = 64
SGU_WIDTH = D_MODEL // 2
SGU_CHUNK = 128
SGU_HEADS = 4
SGU_HEAD_DIM = SGU_WIDTH // SGU_HEADS
D_FF = 4 * D_MODEL
PLE_DIM = 256
IN_COLS = SSM_WIDTH + 2 * SGU_WIDTH + 2 * D_MODEL
DN_ALPHA = float((2 * DEPTH) ** 0.25)
DN_BETA = float((8 * DEPTH) ** -0.25)
LN_EPS = 1e-5

kernel_name = "s5_gmlp_gated_hybrid_step"


def layer_norm(x, g, b):
    xf = x.astype(jnp.float32)
    mu = jnp.mean(xf, axis=-1, keepdims=True)
    xc = xf - mu
    var = jnp.mean(xc * xc, axis=-1, keepdims=True)
    return (xc * lax.rsqrt(var + LN_EPS) * g.astype(jnp.float32) + b.astype(jnp.float32)).astype(x.dtype)


def _complex_affine_combine(e1, e2):
    a1r, a1i, b1r, b1i = e1
    a2r, a2i, b2r, b2i = e2
    ar = a2r * a1r - a2i * a1i
    ai = a2r * a1i + a2i * a1r
    br = a2r * b1r - a2i * b1i + b2r
    bi = a2r * b1i + a2i * b1r + b2i
    return (ar, ai, br, bi)


def s5_mixer(u, h0_re, h0_im, lam_re, lam_im, log_dt, b_re, b_im, c_re, c_im, d_skip, w_glu, b_glu):
    f32 = jnp.float32
    Bn, L, _ = u.shape
    uf = u.astype(f32)
    ug = uf.reshape(Bn, L, SSM_GROUPS, SSM_GROUP)
    lr = lam_re.astype(f32)
    li = lam_im.astype(f32)
    dt = jnp.exp(log_dt.astype(f32))[:, None]
    mag = jnp.exp(lr * dt)
    ab_re = mag * jnp.cos(li * dt)
    ab_im = mag * jnp.sin(li * dt)
    den = lr * lr + li * li
    nr = ab_re - 1.0
    ni = ab_im
    co_re = (nr * lr + ni * li) / den
    co_im = (ni * lr - nr * li) / den
    br = b_re.astype(f32)
    bi = b_im.astype(f32)
    bb_re = co_re[..., None] * br - co_im[..., None] * bi
    bb_im = co_re[..., None] * bi + co_im[..., None] * br
    bu_re = jnp.einsum('blgc,gpc->lbgp', ug, bb_re)
    bu_im = jnp.einsum('blgc,gpc->lbgp', ug, bb_im)
    a_re_t = jnp.broadcast_to(ab_re[None, None], (L, 1, SSM_GROUPS, SSM_STATE))
    a_im_t = jnp.broadcast_to(ab_im[None, None], (L, 1, SSM_GROUPS, SSM_STATE))
    pw_re, pw_im, h_re, h_im = lax.associative_scan(
        _complex_affine_combine, (a_re_t, a_im_t, bu_re, bu_im), axis=0)
    h0r = h0_re.astype(f32)
    h0i = h0_im.astype(f32)
    h_re, h_im = (h_re + pw_re * h0r - pw_im * h0i,
                  h_im + pw_re * h0i + pw_im * h0r)
    y = (jnp.einsum('lbgp,gcp->blgc', h_re, c_re.astype(f32))
         - jnp.einsum('lbgp,gcp->blgc', h_im, c_im.astype(f32)))
    y = y.reshape(Bn, L, SSM_WIDTH) + d_skip.astype(f32) * uf
    y = jax.nn.gelu(y).astype(u.dtype)
    y = y * jax.nn.sigmoid(y @ w_glu + b_glu)
    return y, h_re[-1], h_im[-1]


def spatial_mix(v, w_s, b_s):
    Bn, L, _ = v.shape
    Lp = -(-L // SGU_CHUNK) * SGU_CHUNK
    vp = jnp.pad(v, ((0, 0), (0, Lp - L), (0, 0)))
    vp = vp.reshape(Bn, Lp // SGU_CHUNK, SGU_CHUNK, SGU_HEADS, SGU_HEAD_DIM)
    mask = jnp.tril(jnp.ones((SGU_CHUNK, SGU_CHUNK), dtype=w_s.dtype))
    s = jnp.einsum('hts,bnshd->bnthd', w_s * mask, vp)
    s = s + jnp.transpose(b_s)[None, None, :, :, None]
    return s.reshape(Bn, Lp, SGU_WIDTH)[:, :L]


def decoder_layer(x, p, h0_re, h0_im, w_in, b_in, lam_re, lam_im, log_dt, b_re, b_im, c_re, c_im,
                  d_skip, w_glu, b_glu, sgu_ln_g, sgu_ln_b, sgu_w, sgu_b, w_branch_a, w_branch_b,
                  w_out, b_out, ln1_g, ln1_b, w_up, b_up, w_down, b_down, w_ple, w_ple_gate,
                  b_ple_gate, ln2_g, ln2_b):
    z = x @ w_in + b_in
    u_a, z_b, gate_a, gate_b = jnp.split(
        z, [SSM_WIDTH, SSM_WIDTH + 2 * SGU_WIDTH, SSM_WIDTH + 2 * SGU_WIDTH + D_MODEL], axis=-1)
    y_a, hT_re, hT_im = s5_mixer(u_a, h0_re, h0_im, lam_re, lam_im, log_dt, b_re, b_im,
                                 c_re, c_im, d_skip, w_glu, b_glu)
    u_b, v_b = jnp.split(jax.nn.gelu(z_b), 2, axis=-1)
    v_b = layer_norm(v_b, sgu_ln_g, sgu_ln_b)
    y_b = u_b * spatial_mix(v_b, sgu_w, sgu_b)
    merged = (jax.nn.sigmoid(gate_a) * (y_a @ w_branch_a)
              + jax.nn.sigmoid(gate_b) * (y_b @ w_branch_b))
    mix = merged @ w_out + b_out
    x1 = layer_norm(DN_ALPHA * x + mix, ln1_g, ln1_b)
    ff = jnp.square(jax.nn.relu(x1 @ w_up + b_up)) @ w_down + b_down
    ple = jax.nn.sigmoid(x1 @ w_ple_gate + b_ple_gate) * (p @ w_ple)
    x2 = layer_norm(DN_ALPHA * x1 + ff + ple, ln2_g, ln2_b)
    return x2, hT_re, hT_im, v_b


def setup_inputs(seed: int = 0) -> dict:
    key = jax.random.key(seed)
    ks = iter(jax.random.split(key, 48))

    def nrm(shape, scale):
        return jax.random.normal(next(ks), shape, jnp.float32) * scale

    G, P, C = SSM_GROUPS, SSM_STATE, SSM_GROUP
    n_idx = jnp.arange(P, dtype=jnp.float32)
    return {
        "x_prompt": nrm((BATCH, SEQ, D_MODEL), 1.0),
        "x_sample": nrm((DEC_BATCH, DEC_SEQ, D_MODEL), 1.0),
        "state_ssm_re": nrm((DEPTH, DEC_BATCH, G, P), 0.5),
        "state_ssm_im": nrm((DEPTH, DEC_BATCH, G, P), 0.5),
        "p_prompt": nrm((DEPTH, BATCH, SEQ, PLE_DIM), 1.0),
        "p_sample": nrm((DEPTH, DEC_BATCH, DEC_SEQ, PLE_DIM), 1.0),
        "w_in": nrm((DEPTH, D_MODEL, IN_COLS), D_MODEL ** -0.5),
        "b_in": nrm((DEPTH, IN_COLS), 0.02),
        "ssm_lambda_re": -0.5 + nrm((DEPTH, G, P), 0.01),
        "ssm_lambda_im": math.pi * n_idx[None, None, :] + nrm((DEPTH, G, P), 0.01),
        "ssm_log_dt": jax.random.uniform(next(ks), (DEPTH, G), jnp.float32,
                                         minval=math.log(1e-3), maxval=math.log(1e-1)),
        "ssm_b_re": nrm((DEPTH, G, P, C), (2 * C) ** -0.5),
        "ssm_b_im": nrm((DEPTH, G, P, C), (2 * C) ** -0.5),
        "ssm_c_re": nrm((DEPTH, G, C, P), (2 * P) ** -0.5),
        "ssm_c_im": nrm((DEPTH, G, C, P), (2 * P) ** -0.5),
        "ssm_d": nrm((DEPTH, SSM_WIDTH), 0.5),
        "w_glu": nrm((DEPTH, SSM_WIDTH, SSM_WIDTH), SSM_WIDTH ** -0.5),
        "b_glu": nrm((DEPTH, SSM_WIDTH), 0.02),
        "sgu_ln_g": 1.0 + nrm((DEPTH, SGU_WIDTH), 0.02),
        "sgu_ln_b": nrm((DEPTH, SGU_WIDTH), 0.02),
        "sgu_w": nrm((DEPTH, SGU_HEADS, SGU_CHUNK, SGU_CHUNK), SGU_CHUNK ** -0.5),
        "sgu_b": 1.0 + nrm((DEPTH, SGU_HEADS, SGU_CHUNK), 0.01),
        "w_branch_a": nrm((DEPTH, SSM_WIDTH, D_MODEL), DN_BETA * SSM_WIDTH ** -0.5),
        "w_branch_b": nrm((DEPTH, SGU_WIDTH, D_MODEL), DN_BETA * SGU_WIDTH ** -0.5),
        "w_out": nrm((DEPTH, D_MODEL, D_MODEL), DN_BETA * D_MODEL ** -0.5),
        "b_out": nrm((DEPTH, D_MODEL), 0.02),
        "ln1_g": 1.0 + nrm((DEPTH, D_MODEL), 0.02),
        "ln1_b": nrm((DEPTH, D_MODEL), 0.02),
        "w_up": nrm((DEPTH, D_MODEL, D_FF), DN_BETA * D_MODEL ** -0.5),
        "b_up": nrm((DEPTH, D_FF), 0.02),
        "w_down": nrm((DEPTH, D_FF, D_MODEL), DN_BETA * D_FF ** -0.5),
        "b_down": nrm((DEPTH, D_MODEL), 0.02),
        "w_ple": nrm((DEPTH, PLE_DIM, D_MODEL), PLE_DIM ** -0.5),
        "w_ple_gate": nrm((DEPTH, D_MODEL, D_MODEL), D_MODEL ** -0.5),
        "b_ple_gate": nrm((DEPTH, D_MODEL), 0.02),
        "ln2_g": 1.0 + nrm((DEPTH, D_MODEL), 0.02),
        "ln2_b": nrm((DEPTH, D_MODEL), 0.02),
    }


def reference(x_prompt, x_sample, state_ssm_re, state_ssm_im, p_prompt, p_sample,
              w_in, b_in, ssm_lambda_re, ssm_lambda_im, ssm_log_dt, ssm_b_re, ssm_b_im,
              ssm_c_re, ssm_c_im, ssm_d, w_glu, b_glu, sgu_ln_g, sgu_ln_b, sgu_w, sgu_b,
              w_branch_a, w_branch_b, w_out, b_out, ln1_g, ln1_b, w_up, b_up, w_down, b_down,
              w_ple, w_ple_gate, b_ple_gate, ln2_g, ln2_b):
    xp = x_prompt
    xs = x_sample
    h0p = jnp.zeros((x_prompt.shape[0], SSM_GROUPS, SSM_STATE), jnp.float32)
    pre_l, pim_l, sre_l, sim_l, sv_l = [], [], [], [], []
    for i in range(DEPTH):
        w = (w_in[i], b_in[i], ssm_lambda_re[i], ssm_lambda_im[i], ssm_log_dt[i], ssm_b_re[i],
             ssm_b_im[i], ssm_c_re[i], ssm_c_im[i], ssm_d[i], w_glu[i], b_glu[i], sgu_ln_g[i],
             sgu_ln_b[i], sgu_w[i], sgu_b[i], w_branch_a[i], w_branch_b[i], w_out[i], b_out[i],
             ln1_g[i], ln1_b[i], w_up[i], b_up[i], w_down[i], b_down[i], w_ple[i],
             w_ple_gate[i], b_ple_gate[i], ln2_g[i], ln2_b[i])
        xp, hpr, hpi, _ = decoder_layer(xp, p_prompt[i], h0p, h0p, *w)
        xs, hsr, hsi, vs = decoder_layer(xs, p_sample[i], state_ssm_re[i], state_ssm_im[i], *w)
        pre_l.append(hpr)
        pim_l.append(hpi)
        sre_l.append(hsr)
        sim_l.append(hsi)
        sv_l.append(vs)
    ssm_re_prompt = jnp.stack(pre_l)
    ssm_im_prompt = jnp.stack(pim_l)
    ssm_re_sample = jnp.stack(sre_l)
    ssm_im_sample = jnp.stack(sim_l)
    sgu_v_sample = jnp.stack(sv_l)
    return (xp, xs, ssm_re_prompt, ssm_im_prompt, ssm_re_sample, ssm_im_sample, sgu_v_sample)
```

```python
import functools
import math

import jax
import jax.numpy as jnp
from jax import lax
from jax.experimental import pallas as pl
from jax.experimental.pallas import tpu as pltpu

F32 = jnp.float32
BF16 = jnp.bfloat16

LN_EPS = 1e-5
SSM_GROUP = 16
SGU_CHUNK = 128
GROUPS_PER_BLOCK = 8
SUBLANES = 8
VMEM_LIMIT_BYTES = 56 * 1024 * 1024


def _gelu(x):
    c = math.sqrt(2.0 / math.pi)
    return x * (0.5 * (1.0 + jnp.tanh(c * (x + 0.044715 * (x * x * x)))))


def _sigmoid(x):
    return 1.0 / (1.0 + jnp.exp(-x))


def _layer_norm(x, g, b):
    mu = jnp.mean(x, axis=-1, keepdims=True)
    xc = x - mu
    var = jnp.mean(xc * xc, axis=-1, keepdims=True)
    return xc * lax.rsqrt(var + LN_EPS) * g + b


def _dot(a, b):
    return jnp.dot(a, b, preferred_element_type=F32)


def _const_spec(shape):
    zeros = (0,) * len(shape)
    return pl.BlockSpec(shape, lambda i: zeros, pipeline_mode=pl.Buffered(1))


def _row_spec(rows, cols):
    return pl.BlockSpec((rows, cols), lambda i: (i, 0))


def _params(semantics):
    return pltpu.CompilerParams(dimension_semantics=(semantics,), vmem_limit_bytes=VMEM_LIMIT_BYTES)


def _discretise_kernel(lr_ref, li_ref, ldt_ref, brt_ref, bit_ref, are_ref, aim_ref, bbr_ref, bbi_ref):
    lr = lr_ref[...]
    li = li_ref[...]
    dt = jnp.exp(ldt_ref[...])
    mag = jnp.exp(lr * dt)
    ab_re = mag * jnp.cos(li * dt)
    ab_im = mag * jnp.sin(li * dt)
    den = lr * lr + li * li
    nr = ab_re - 1.0
    ni = ab_im
    co_re = (nr * lr + ni * li) / den
    co_im = (ni * lr - nr * li) / den
    br = brt_ref[...]
    bi = bit_ref[...]
    are_ref[...] = ab_re
    aim_ref[...] = ab_im
    bbr_ref[...] = co_re * br - co_im * bi
    bbi_ref[...] = co_re * bi + co_im * br


def _discretise(lam_re, lam_im, log_dt, b_re, b_im):
    g, p, c = b_re.shape
    n = g * p
    row = lambda a: a.reshape(1, n)
    ldt = jnp.broadcast_to(log_dt[:, None], (g, p))
    brt = b_re.reshape(n, c).T
    bit = b_im.reshape(n, c).T
    out_shape = (jax.ShapeDtypeStruct((1, n), F32), jax.ShapeDtypeStruct((1, n), F32),
                 jax.ShapeDtypeStruct((c, n), F32), jax.ShapeDtypeStruct((c, n), F32))
    return pl.pallas_call(_discretise_kernel, out_shape=out_shape, name="s5_discretise")(
        row(lam_re), row(lam_im), row(ldt), brt, bit)


def _block_diag_b(bbt, g, p):
    c = bbt.shape[0]
    nb = g // GROUPS_PER_BLOCK
    b4 = bbt.reshape(c, nb, GROUPS_PER_BLOCK, p)
    eye = jnp.eye(GROUPS_PER_BLOCK, dtype=bool)
    blk = jnp.where(eye[None, :, None, :, None], jnp.transpose(b4, (1, 0, 2, 3))[:, None], 0.0)
    return blk.reshape(nb, GROUPS_PER_BLOCK * c, GROUPS_PER_BLOCK * p)


def _block_diag_c(cmat, g, p):
    c = cmat.shape[1]
    nb = g // GROUPS_PER_BLOCK
    c4 = cmat.reshape(nb, GROUPS_PER_BLOCK, c, p)
    eye = jnp.eye(GROUPS_PER_BLOCK, dtype=bool)
    blk = jnp.where(eye[None, :, None, :, None], jnp.transpose(c4, (0, 3, 1, 2))[:, None], 0.0)
    return blk.reshape(nb, GROUPS_PER_BLOCK * p, GROUPS_PER_BLOCK * c)


def _inproj_kernel(x_ref, w_ref, b_ref, lng_ref, lnb_ref, ua_ref, ub_ref, vb_ref, gates_ref, *, ssm_w, sgu_w):
    xb = x_ref[...].astype(BF16)

    def seg(lo, hi):
        return _dot(xb, w_ref[:, lo:hi]) + b_ref[:, lo:hi]

    ua_ref[...] = seg(0, ssm_w)
    ub_ref[...] = _gelu(seg(ssm_w, ssm_w + sgu_w)).astype(BF16)
    v = _gelu(seg(ssm_w + sgu_w, ssm_w + 2 * sgu_w))
    vb_ref[...] = _layer_norm(v, lng_ref[...], lnb_ref[...])
    gates_ref[...] = _sigmoid(seg(ssm_w + 2 * sgu_w, w_ref.shape[1])).astype(BF16)


def _inproj(x2d, w_in, b_in, ln_g, ln_b, ssm_w, sgu_w, tm):
    m, d = x2d.shape
    cols = w_in.shape[1]
    gate_w = cols - ssm_w - 2 * sgu_w
    out_shape = (jax.ShapeDtypeStruct((m, ssm_w), F32), jax.ShapeDtypeStruct((m, sgu_w), BF16),
                 jax.ShapeDtypeStruct((m, sgu_w), F32), jax.ShapeDtypeStruct((m, gate_w), BF16))
    return pl.pallas_call(
        functools.partial(_inproj_kernel, ssm_w=ssm_w, sgu_w=sgu_w),
        grid=(m // tm,),
        in_specs=[_row_spec(tm, d), _const_spec((d, cols)), _const_spec((1, cols)),
                  _const_spec((1, sgu_w)), _const_spec((1, sgu_w))],
        out_specs=(_row_spec(tm, ssm_w), _row_spec(tm, sgu_w), _row_spec(tm, sgu_w), _row_spec(tm, gate_w)),
        out_shape=out_shape,
        compiler_params=_params("parallel"),
        name="in_proj",
    )(x2d, w_in, b_in, ln_g, ln_b)


def _s5_kernel(u_ref, h0r_ref, h0i_ref, ar_ref, ai_ref, bblk_ref, cre_ref, cim_ref, d_ref, wglu_ref, bglu_ref,
               ya_ref, hr_ref, hi_ref, s_ref, y_ref, *, steps, bsub):
    @pl.when(pl.program_id(0) == 0)
    def _():
        hr_ref[...] = h0r_ref[...]
        hi_ref[...] = h0i_ref[...]

    n_blocks, slab, two_n = bblk_ref.shape
    n = two_n // 2
    u = u_ref[...]
    ub = u.astype(BF16)
    for j in range(n_blocks):
        cols = slice(j * n, (j + 1) * n)
        a_re = jnp.broadcast_to(ar_ref[:, cols], (bsub, n))
        a_im = jnp.broadcast_to(ai_ref[:, cols], (bsub, n))
        s_ref[...] = _dot(ub[:, j * slab:(j + 1) * slab], bblk_ref[j])
        h_re = hr_ref[:, cols]
        h_im = hi_ref[:, cols]
        if steps == 1:
            n_re = a_re * h_re - a_im * h_im + s_ref[:, :n]
            n_im = a_re * h_im + a_im * h_re + s_ref[:, n:]
            s_ref[:, :n] = n_re
            s_ref[:, n:] = n_im
            h_re, h_im = n_re, n_im
        else:
            def body(t, carry):
                c_re, c_im = carry
                rows = pl.ds(pl.multiple_of(t * bsub, bsub), bsub)
                n_re = a_re * c_re - a_im * c_im + s_ref[rows, :n]
                n_im = a_re * c_im + a_im * c_re + s_ref[rows, n:]
                s_ref[rows, :n] = n_re
                s_ref[rows, n:] = n_im
                return n_re, n_im

            h_re, h_im = lax.fori_loop(0, steps, body, (h_re, h_im), unroll=8)
        hr_ref[:, cols] = h_re
        hi_ref[:, cols] = h_im
        y_ref[:, j * slab:(j + 1) * slab] = (_dot(s_ref[:, :n].astype(BF16), cre_ref[j])
                                             - _dot(s_ref[:, n:].astype(BF16), cim_ref[j]))
    y = _gelu(y_ref[...] + d_ref[...] * u)
    ya_ref[...] = (y * _sigmoid(_dot(y.astype(BF16), wglu_ref[...]) + bglu_ref[...])).astype(BF16)


def _s5(u_tb, h0_re, h0_im, a_re, a_im, bblk, cre, cim, d_skip, w_glu, b_glu, steps_total, bsub, steps):
    rows_total, w = u_tb.shape
    gp = a_re.shape[1]
    rows = steps * bsub
    nb, slab, two_n = bblk.shape
    out_shape = (jax.ShapeDtypeStruct((rows_total, w), BF16),
                 jax.ShapeDtypeStruct((bsub, gp), F32), jax.ShapeDtypeStruct((bsub, gp), F32))
    state_spec = pl.BlockSpec((bsub, gp), lambda i: (0, 0))
    return pl.pallas_call(
        functools.partial(_s5_kernel, steps=steps, bsub=bsub),
        grid=(steps_total // steps,),
        in_specs=[_row_spec(rows, w), _const_spec((bsub, gp)), _const_spec((bsub, gp)),
                  _const_spec((1, gp)), _const_spec((1, gp)),
                  _const_spec(bblk.shape), _const_spec(cre.shape), _const_spec(cim.shape),
                  _const_spec((1, w)), _const_spec((w, w)), _const_spec((1, w))],
        out_specs=(_row_spec(rows, w), state_spec, state_spec),
        out_shape=out_shape,
        scratch_shapes=[pltpu.VMEM((rows, two_n), F32), pltpu.VMEM((rows, w), F32)],
        compiler_params=_params("arbitrary"),
        name="s5_scan",
    )(u_tb, h0_re, h0_im, a_re, a_im, bblk, cre, cim, d_skip, w_glu, b_glu)


def _mix_kernel(ub_ref, vb_ref, ya_ref, gates_ref, x_ref, ws_ref, srow_ref, sbias_ref, wpa_ref, wpb_ref,
                wout_ref, bout_ref, g1_ref, b1_ref, x1_ref, yb_ref, *, alpha, single_position):
    tm, sgu_w = ub_ref.shape
    heads, chunk, _ = ws_ref.shape
    hd = sgu_w // heads
    if single_position:
        s = vb_ref[...] * srow_ref[...] + sbias_ref[0:1, :]
        yb_ref[...] = (ub_ref[...].astype(F32) * s).astype(BF16)
    else:
        keep = (lax.broadcasted_iota(jnp.int32, (chunk, chunk), 0)
                >= lax.broadcasted_iota(jnp.int32, (chunk, chunk), 1))
        for h in range(heads):
            wm = jnp.where(keep, ws_ref[h], 0.0).astype(BF16)
            hc = slice(h * hd, (h + 1) * hd)
            for c in range(tm // chunk):
                rc = slice(c * chunk, (c + 1) * chunk)
                s = _dot(wm, vb_ref[rc, hc].astype(BF16)) + sbias_ref[:, hc]
                yb_ref[rc, hc] = (ub_ref[rc, hc].astype(F32) * s).astype(BF16)
    d = x_ref.shape[1]
    ga = gates_ref[:, :d].astype(F32)
    gb = gates_ref[:, d:].astype(F32)
    merged = ga * _dot(ya_ref[...], wpa_ref[...]) + gb * _dot(yb_ref[...], wpb_ref[...])
    mix = _dot(merged.astype(BF16), wout_ref[...]) + bout_ref[...]
    x1_ref[...] = _layer_norm(alpha * x_ref[...] + mix, g1_ref[...], b1_ref[...])


def _mix(u_b, v_b, y_a, gates, x2d, sgu_w_mat, s_row, s_bias, w_pa, w_pb, w_out, b_out, g1, b1, alpha, tm,
         single_position):
    m, d = x2d.shape
    sgu_w = u_b.shape[1]
    return pl.pallas_call(
        functools.partial(_mix_kernel, alpha=alpha, single_position=single_position),
        grid=(m // tm,),
        in_specs=[_row_spec(tm, sgu_w), _row_spec(tm, sgu_w), _row_spec(tm, y_a.shape[1]),
                  _row_spec(tm, gates.shape[1]), _row_spec(tm, d),
                  _const_spec(sgu_w_mat.shape), _const_spec(s_row.shape), _const_spec(s_bias.shape),
                  _const_spec(w_pa.shape), _const_spec(w_pb.shape), _const_spec(w_out.shape),
                  _const_spec((1, d)), _const_spec((1, d)), _const_spec((1, d))],
        out_specs=_row_spec(tm, d),
        out_shape=jax.ShapeDtypeStruct((m, d), F32),
        scratch_shapes=[pltpu.VMEM((tm, sgu_w), BF16)],
        compiler_params=_params("parallel"),
        name="mix_merge_ln1",
    )(u_b, v_b, y_a, gates, x2d, sgu_w_mat, s_row, s_bias, w_pa, w_pb, w_out, b_out, g1, b1)


def _ffn_kernel(x1_ref, p_ref, wup_ref, bup_ref, wdn_ref, bdn_ref, wple_ref, wpg_ref, bpg_ref, g2_ref, b2_ref,
                x2_ref, *, alpha):
    x1 = x1_ref[...]
    x1b = x1.astype(BF16)
    hidden = jnp.square(jnp.maximum(_dot(x1b, wup_ref[...]) + bup_ref[...], 0.0))
    ff = _dot(hidden.astype(BF16), wdn_ref[...]) + bdn_ref[...]
    ple = _sigmoid(_dot(x1b, wpg_ref[...]) + bpg_ref[...]) * _dot(p_ref[...].astype(BF16), wple_ref[...])
    x2_ref[...] = _layer_norm(alpha * x1 + ff + ple, g2_ref[...], b2_ref[...])


def _ffn(x1, p2d, w_up, b_up, w_down, b_down, w_ple, w_pg, b_pg, g2, b2, alpha, tm):
    m, d = x1.shape
    return pl.pallas_call(
        functools.partial(_ffn_kernel, alpha=alpha),
        grid=(m // tm,),
        in_specs=[_row_spec(tm, d), _row_spec(tm, p2d.shape[1]),
                  _const_spec(w_up.shape), _const_spec((1, w_up.shape[1])),
                  _const_spec(w_down.shape), _const_spec((1, d)),
                  _const_spec(w_ple.shape), _const_spec(w_pg.shape), _const_spec((1, d)),
                  _const_spec((1, d)), _const_spec((1, d))],
        out_specs=_row_spec(tm, d),
        out_shape=jax.ShapeDtypeStruct((m, d), F32),
        compiler_params=_params("parallel"),
        name="ffn_ln2",
    )(x1, p2d, w_up, b_up, w_down, b_down, w_ple, w_pg, b_pg, g2, b2)


def _decoder_layer(x, p, h0_re, h0_im, wts, alpha, row_tile):
    bn, ln, d = x.shape
    m = bn * ln
    ssm_w = wts["d_skip"].shape[1]
    sgu_w = wts["ln_g"].shape[1]
    x2d = x.reshape(m, d)
    tm = min(row_tile, m)
    u_a, u_b, v_b, gates = _inproj(x2d, wts["w_in"], wts["b_in"], wts["ln_g"], wts["ln_b"], ssm_w, sgu_w, tm)

    u_tb = jnp.transpose(u_a.reshape(bn, ln, ssm_w), (1, 0, 2)).reshape(m, ssm_w)
    steps = min(SGU_CHUNK, ln)
    y_tb, h_re, h_im = _s5(u_tb, h0_re, h0_im, wts["a_re"], wts["a_im"], wts["bblk"], wts["cre"], wts["cim"],
                           wts["d_skip"], wts["w_glu"], wts["b_glu"], ln, bn, steps)
    y_a = jnp.transpose(y_tb.reshape(ln, bn, ssm_w), (1, 0, 2)).reshape(m, ssm_w)

    x1 = _mix(u_b, v_b, y_a, gates, x2d, wts["sgu_w"], wts["s_row"], wts["s_bias"], wts["w_pa"], wts["w_pb"],
              wts["w_out"], wts["b_out"], wts["ln1_g"], wts["ln1_b"], alpha, tm, single_position=(ln == 1))
    x2 = _ffn(x1, p.reshape(m, p.shape[-1]), wts["w_up"], wts["b_up"], wts["w_down"], wts["b_down"],
              wts["w_ple"], wts["w_pg"], wts["b_pg"], wts["ln2_g"], wts["ln2_b"], alpha, tm)
    return x2.reshape(bn, ln, d), h_re, h_im, v_b.reshape(bn, ln, sgu_w)


def kernel(x_prompt, x_sample, state_ssm_re, state_ssm_im, p_prompt, p_sample, w_in, b_in, ssm_lambda_re, ssm_lambda_im, ssm_log_dt, ssm_b_re, ssm_b_im, ssm_c_re, ssm_c_im, ssm_d, w_glu, b_glu, sgu_ln_g, sgu_ln_b, sgu_w, sgu_b, w_branch_a, w_branch_b, w_out, b_out, ln1_g, ln1_b, w_up, b_up, w_down, b_down, w_ple, w_ple_gate, b_ple_gate, ln2_g, ln2_b):
    depth = w_in.shape[0]
    g, p_state = ssm_lambda_re.shape[1:]
    heads, chunk = sgu_b.shape[1:]
    sgu_width = sgu_ln_g.shape[1]
    hd = sgu_width // heads
    alpha = float((2 * depth) ** 0.25)
    row = lambda a: a.reshape(1, -1)

    xp, xs = x_prompt, x_sample
    bp = x_prompt.shape[0]
    bs = x_sample.shape[0]
    zeros_p = jnp.zeros((bp, g * p_state), F32)
    outs = {k: [] for k in ("pre", "pim", "sre", "sim", "sv")}
    for i in range(depth):
        a_re, a_im, bbr, bbi = _discretise(ssm_lambda_re[i], ssm_lambda_im[i], ssm_log_dt[i],
                                           ssm_b_re[i], ssm_b_im[i])
        bblk = jnp.concatenate([_block_diag_b(bbr, g, p_state), _block_diag_b(bbi, g, p_state)],
                               axis=-1).astype(BF16)
        wts = dict(
            w_in=w_in[i].astype(BF16), b_in=row(b_in[i]), ln_g=row(sgu_ln_g[i]), ln_b=row(sgu_ln_b[i]),
            a_re=a_re, a_im=a_im, bblk=bblk,
            cre=_block_diag_c(ssm_c_re[i], g, p_state).astype(BF16),
            cim=_block_diag_c(ssm_c_im[i], g, p_state).astype(BF16),
            d_skip=row(ssm_d[i]), w_glu=w_glu[i].astype(BF16), b_glu=row(b_glu[i]),
            sgu_w=sgu_w[i],
            s_row=jnp.repeat(sgu_w[i][:, 0, 0], hd).reshape(1, sgu_width),
            s_bias=jnp.repeat(sgu_b[i].T, hd, axis=1),
            w_pa=w_branch_a[i].astype(BF16), w_pb=w_branch_b[i].astype(BF16),
            w_out=w_out[i].astype(BF16), b_out=row(b_out[i]), ln1_g=row(ln1_g[i]), ln1_b=row(ln1_b[i]),
            w_up=w_up[i].astype(BF16), b_up=row(b_up[i]), w_down=w_down[i].astype(BF16), b_down=row(b_down[i]),
            w_ple=w_ple[i].astype(BF16), w_pg=w_ple_gate[i].astype(BF16), b_pg=row(b_ple_gate[i]),
            ln2_g=row(ln2_g[i]), ln2_b=row(ln2_b[i]),
        )
        xp, hpr, hpi, _ = _decoder_layer(xp, p_prompt[i], zeros_p, zeros_p, wts, alpha, row_tile=512)
        xs, hsr, hsi, vs = _decoder_layer(xs, p_sample[i], state_ssm_re[i].reshape(bs, g * p_state),
                                          state_ssm_im[i].reshape(bs, g * p_state), wts, alpha, row_tile=512)
        outs["pre"].append(hpr.reshape(bp, g, p_state))
        outs["pim"].append(hpi.reshape(bp, g, p_state))
        outs["sre"].append(hsr.reshape(bs, g, p_state))
        outs["sim"].append(hsi.reshape(bs, g, p_state))
        outs["sv"].append(vs)
    return (xp, xs, jnp.stack(outs["pre"]), jnp.stack(outs["pim"]), jnp.stack(outs["sre"]),
            jnp.stack(outs["sim"]), jnp.stack(outs["sv"]))
```

```python
import functools
import math

import jax
import jax.numpy as jnp
from jax import lax
from jax.experimental import pallas as pl
from jax.experimental.pallas import tpu as pltpu

F32 = jnp.float32
BF16 = jnp.bfloat16

LN_EPS = 1e-5
SGU_CHUNK = 128
GROUPS_PER_BLOCK = 8
S5_CHUNK = 4
S5_CHUNKS_PER_STEP = 64
VMEM_LIMIT_BYTES = 56 * 1024 * 1024


def _gelu(x):
    c = math.sqrt(2.0 / math.pi)
    return x * (0.5 * (1.0 + jnp.tanh(c * (x + 0.044715 * (x * x * x)))))


def _sigmoid(x):
    return 1.0 / (1.0 + jnp.exp(-x))


def _layer_norm(x, g, b):
    mu = jnp.mean(x, axis=-1, keepdims=True)
    xc = x - mu
    var = jnp.mean(xc * xc, axis=-1, keepdims=True)
    return xc * lax.rsqrt(var + LN_EPS) * g + b


def _dot(a, b):
    return jnp.dot(a, b, preferred_element_type=F32)


def _const_spec(shape):
    zeros = (0,) * len(shape)
    return pl.BlockSpec(shape, lambda i: zeros, pipeline_mode=pl.Buffered(1))


def _row_spec(rows, cols):
    return pl.BlockSpec((rows, cols), lambda i: (i, 0))


def _params(semantics):
    return pltpu.CompilerParams(dimension_semantics=(semantics,), vmem_limit_bytes=VMEM_LIMIT_BYTES)


def _discretise_kernel(lr_ref, li_ref, ldt_ref, br_ref, bi_ref, cr_ref, ci_ref,
                       apr_ref, api_ref, wr_ref, wi_ref, cer_ref, cei_ref, km_ref):
    q = wr_ref.shape[0]
    lr = lr_ref[...]
    li = li_ref[...]
    dt = jnp.exp(ldt_ref[...])
    mag = jnp.exp(lr * dt)
    ar = mag * jnp.cos(li * dt)
    ai = mag * jnp.sin(li * dt)
    den = lr * lr + li * li
    nr = ar - 1.0
    co_re = (nr * lr + ai * li) / den
    co_im = (ai * lr - nr * li) / den
    br = br_ref[...]
    bi = bi_ref[...]
    c_re = cr_ref[...]
    c_im = ci_ref[...]
    c_re_b = c_re.astype(BF16)
    c_im_b = c_im.astype(BF16)
    wr = co_re * br - co_im * bi
    wi = co_re * bi + co_im * br
    pr, pi = ar, ai
    for d in range(q):
        wr_ref[d] = wr
        wi_ref[d] = wi
        km_ref[d] = (jnp.einsum("gcp,gdp->gcd", c_re_b, wr.astype(BF16), preferred_element_type=F32)
                     - jnp.einsum("gcp,gdp->gcd", c_im_b, wi.astype(BF16), preferred_element_type=F32))
        apr_ref[d] = pr
        api_ref[d] = pi
        cer_ref[d] = c_re * pr - c_im * pi
        cei_ref[d] = c_re * pi + c_im * pr
        wr, wi = ar * wr - ai * wi, ar * wi + ai * wr
        pr, pi = ar * pr - ai * pi, ar * pi + ai * pr


def _discretise(lam_re, lam_im, log_dt, b_re, b_im, c_re, c_im, q):
    g, p, c = b_re.shape
    gcp = lambda a: jnp.broadcast_to(a[:, None, :], (g, c, p))
    ldt = jnp.broadcast_to(log_dt[:, None], (g, p))
    t = lambda a: jnp.transpose(a, (0, 2, 1))
    sds = lambda *s: jax.ShapeDtypeStruct(s, F32)
    out_shape = (sds(q, g, c, p),) * 6 + (sds(q, g, c, c),)
    return pl.pallas_call(_discretise_kernel, out_shape=out_shape, name="s5_discretise")(
        gcp(lam_re), gcp(lam_im), gcp(ldt), t(b_re), t(b_im), c_re, c_im)


def _block_diag(x):
    g, r, s = x.shape
    nb = g // GROUPS_PER_BLOCK
    x5 = x.reshape(nb, GROUPS_PER_BLOCK, r, 1, s)
    eye = jnp.eye(GROUPS_PER_BLOCK, dtype=bool).reshape(1, GROUPS_PER_BLOCK, 1, GROUPS_PER_BLOCK, 1)
    return jnp.where(eye, x5, 0.0).reshape(nb, GROUPS_PER_BLOCK * r, GROUPS_PER_BLOCK * s)


def _chunk_operators(ap_re, ap_im, w_re, w_im, ce_re, ce_im, km, q):
    t = lambda a: jnp.transpose(a, (0, 2, 1))
    g, _, p = ap_re.shape[1:]
    a_q = (ap_re[q - 1][:, 0, :].reshape(1, g * p), ap_im[q - 1][:, 0, :].reshape(1, g * p))
    b_in = jnp.concatenate(
        [jnp.concatenate([_block_diag(w_re[q - 1 - i]), _block_diag(w_im[q - 1 - i])], axis=-1)
         for i in range(q)], axis=1)
    c_out_re = jnp.concatenate([_block_diag(t(ce_re[i])) for i in range(q)], axis=-1)
    c_out_im = jnp.concatenate([_block_diag(t(ce_im[i])) for i in range(q)], axis=-1)
    kd = [_block_diag(t(km[d])) for d in range(q)]
    zero = jnp.zeros_like(kd[0])
    k_mix = jnp.concatenate(
        [jnp.concatenate([kd[i - ip] if ip <= i else zero for i in range(q)], axis=-1) for ip in range(q)], axis=1)
    return dict(a_re=a_q[0], a_im=a_q[1], b_in=b_in.astype(BF16), c_re=c_out_re.astype(BF16),
                c_im=c_out_im.astype(BF16), k_mix=k_mix.astype(BF16))


def _inproj_kernel(x_ref, w_ref, b_ref, lng_ref, lnb_ref, ua_ref, ub_ref, vb_ref, gates_ref, *, ssm_w, sgu_w):
    xb = x_ref[...].astype(BF16)

    def seg(lo, hi):
        return _dot(xb, w_ref[:, lo:hi]) + b_ref[:, lo:hi]

    ua_ref[...] = seg(0, ssm_w)
    ub_ref[...] = _gelu(seg(ssm_w, ssm_w + sgu_w)).astype(BF16)
    v = _gelu(seg(ssm_w + sgu_w, ssm_w + 2 * sgu_w))
    vb_ref[...] = _layer_norm(v, lng_ref[...], lnb_ref[...])
    gates_ref[...] = _sigmoid(seg(ssm_w + 2 * sgu_w, w_ref.shape[1])).astype(BF16)


def _inproj(x2d, w_in, b_in, ln_g, ln_b, ssm_w, sgu_w, tm):
    m, d = x2d.shape
    cols = w_in.shape[1]
    gate_w = cols - ssm_w - 2 * sgu_w
    out_shape = (jax.ShapeDtypeStruct((m, ssm_w), F32), jax.ShapeDtypeStruct((m, sgu_w), BF16),
                 jax.ShapeDtypeStruct((m, sgu_w), F32), jax.ShapeDtypeStruct((m, gate_w), BF16))
    return pl.pallas_call(
        functools.partial(_inproj_kernel, ssm_w=ssm_w, sgu_w=sgu_w),
        grid=(m // tm,),
        in_specs=[_row_spec(tm, d), _const_spec((d, cols)), _const_spec((1, cols)),
                  _const_spec((1, sgu_w)), _const_spec((1, sgu_w))],
        out_specs=(_row_spec(tm, ssm_w), _row_spec(tm, sgu_w), _row_spec(tm, sgu_w), _row_spec(tm, gate_w)),
        out_shape=out_shape,
        compiler_params=_params("parallel"),
        name="in_proj",
    )(x2d, w_in, b_in, ln_g, ln_b)


def _s5_kernel(u_ref, h0r_ref, h0i_ref, ar_ref, ai_ref, bin_ref, cre_ref, cim_ref, kmix_ref, d_ref, wglu_ref,
               bglu_ref, ya_ref, hr_ref, hi_ref, v_ref, y_ref):
    @pl.when(pl.program_id(0) == 0)
    def _():
        hr_ref[...] = h0r_ref[...]
        hi_ref[...] = h0i_ref[...]

    nk, q, bsub, w = u_ref.shape
    n_blocks, _, two_n = bin_ref.shape
    n = two_n // 2
    slab = w // n_blocks
    rows = nk * bsub
    for j in range(n_blocks):
        cols = slice(j * n, (j + 1) * n)
        lanes = slice(j * slab, (j + 1) * slab)
        a_re = jnp.broadcast_to(ar_ref[:, cols], (bsub, n))
        a_im = jnp.broadcast_to(ai_ref[:, cols], (bsub, n))
        xcat = jnp.concatenate([u_ref[:, i, :, lanes].reshape(rows, slab) for i in range(q)],
                               axis=-1).astype(BF16)
        v_ref[...] = _dot(xcat, bin_ref[j])
        h_re = hr_ref[:, cols]
        h_im = hi_ref[:, cols]
        if nk == 1:
            n_re = a_re * h_re - a_im * h_im + v_ref[:, :n]
            n_im = a_re * h_im + a_im * h_re + v_ref[:, n:]
            v_ref[:, :n] = h_re
            v_ref[:, n:] = h_im
            h_re, h_im = n_re, n_im
        else:
            def body(k, carry):
                c_re, c_im = carry
                r = pl.ds(pl.multiple_of(k * bsub, bsub), bsub)
                n_re = a_re * c_re - a_im * c_im + v_ref[r, :n]
                n_im = a_re * c_im + a_im * c_re + v_ref[r, n:]
                v_ref[r, :n] = c_re
                v_ref[r, n:] = c_im
                return n_re, n_im

            h_re, h_im = lax.fori_loop(0, nk, body, (h_re, h_im), unroll=8)
        hr_ref[:, cols] = h_re
        hi_ref[:, cols] = h_im
        y = (_dot(v_ref[:, :n].astype(BF16), cre_ref[j]) - _dot(v_ref[:, n:].astype(BF16), cim_ref[j])
             + _dot(xcat, kmix_ref[j]))
        for i in range(q):
            y_ref[:, i, :, lanes] = y[:, i * slab:(i + 1) * slab].reshape(nk, bsub, slab)
    m = nk * q * bsub
    y = _gelu(y_ref[...].reshape(m, w) + d_ref[...] * u_ref[...].reshape(m, w))
    ya = y * _sigmoid(_dot(y.astype(BF16), wglu_ref[...]) + bglu_ref[...])
    ya_ref[...] = ya.reshape(nk, q, bsub, w).astype(BF16)


def _s5(u4, h0_re, h0_im, ops, d_skip, w_glu, b_glu, nk):
    nk_total, q, bsub, w = u4.shape
    gp = ops["a_re"].shape[1]
    two_n = ops["b_in"].shape[2]
    blk = pl.BlockSpec((nk, q, bsub, w), lambda i: (i, 0, 0, 0))
    state_spec = pl.BlockSpec((bsub, gp), lambda i: (0, 0))
    out_shape = (jax.ShapeDtypeStruct(u4.shape, BF16),
                 jax.ShapeDtypeStruct((bsub, gp), F32), jax.ShapeDtypeStruct((bsub, gp), F32))
    return pl.pallas_call(
        _s5_kernel,
        grid=(nk_total // nk,),
        in_specs=[blk, _const_spec((bsub, gp)), _const_spec((bsub, gp)),
                  _const_spec((1, gp)), _const_spec((1, gp)),
                  _const_spec(ops["b_in"].shape), _const_spec(ops["c_re"].shape), _const_spec(ops["c_im"].shape),
                  _const_spec(ops["k_mix"].shape),
                  _const_spec((1, w)), _const_spec((w, w)), _const_spec((1, w))],
        out_specs=(blk, state_spec, state_spec),
        out_shape=out_shape,
        scratch_shapes=[pltpu.VMEM((nk * bsub, two_n), F32), pltpu.VMEM((nk, q, bsub, w), F32)],
        compiler_params=_params("arbitrary"),
        name="s5_scan",
    )(u4, h0_re, h0_im, ops["a_re"], ops["a_im"], ops["b_in"], ops["c_re"], ops["c_im"], ops["k_mix"],
      d_skip, w_glu, b_glu)


def _mix_kernel(ub_ref, vb_ref, ya_ref, gates_ref, x_ref, ws_ref, srow_ref, sbias_ref, wpa_ref, wpb_ref,
                wout_ref, bout_ref, g1_ref, b1_ref, x1_ref, yb_ref, *, alpha, single_position):
    tm, sgu_w = ub_ref.shape
    heads, chunk, _ = ws_ref.shape
    hd = sgu_w // heads
    if single_position:
        s = vb_ref[...] * srow_ref[...] + sbias_ref[0:1, :]
        yb_ref[...] = (ub_ref[...].astype(F32) * s).astype(BF16)
    else:
        keep = (lax.broadcasted_iota(jnp.int32, (chunk, chunk), 0)
                >= lax.broadcasted_iota(jnp.int32, (chunk, chunk), 1))
        for h in range(heads):
            wm = jnp.where(keep, ws_ref[h], 0.0).astype(BF16)
            hc = slice(h * hd, (h + 1) * hd)
            for c in range(tm // chunk):
                rc = slice(c * chunk, (c + 1) * chunk)
                s = _dot(wm, vb_ref[rc, hc].astype(BF16)) + sbias_ref[:, hc]
                yb_ref[rc, hc] = (ub_ref[rc, hc].astype(F32) * s).astype(BF16)
    d = x_ref.shape[1]
    ga = gates_ref[:, :d].astype(F32)
    gb = gates_ref[:, d:].astype(F32)
    merged = ga * _dot(ya_ref[...], wpa_ref[...]) + gb * _dot(yb_ref[...], wpb_ref[...])
    mix = _dot(merged.astype(BF16), wout_ref[...]) + bout_ref[...]
    x1_ref[...] = _layer_norm(alpha * x_ref[...] + mix, g1_ref[...], b1_ref[...])


def _mix(u_b, v_b, y_a, gates, x2d, sgu_w_mat, s_row, s_bias, w_pa, w_pb, w_out, b_out, g1, b1, alpha, tm,
         single_position):
    m, d = x2d.shape
    sgu_w = u_b.shape[1]
    return pl.pallas_call(
        functools.partial(_mix_kernel, alpha=alpha, single_position=single_position),
        grid=(m // tm,),
        in_specs=[_row_spec(tm, sgu_w), _row_spec(tm, sgu_w), _row_spec(tm, y_a.shape[1]),
                  _row_spec(tm, gates.shape[1]), _row_spec(tm, d),
                  _const_spec(sgu_w_mat.shape), _const_spec(s_row.shape), _const_spec(s_bias.shape),
                  _const_spec(w_pa.shape), _const_spec(w_pb.shape), _const_spec(w_out.shape),
                  _const_spec((1, d)), _const_spec((1, d)), _const_spec((1, d))],
        out_specs=_row_spec(tm, d),
        out_shape=jax.ShapeDtypeStruct((m, d), F32),
        scratch_shapes=[pltpu.VMEM((tm, sgu_w), BF16)],
        compiler_params=_params("parallel"),
        name="mix_merge_ln1",
    )(u_b, v_b, y_a, gates, x2d, sgu_w_mat, s_row, s_bias, w_pa, w_pb, w_out, b_out, g1, b1)


def _ffn_kernel(x1_ref, p_ref, wup_ref, bup_ref, wdn_ref, bdn_ref, wple_ref, wpg_ref, bpg_ref, g2_ref, b2_ref,
                x2_ref, *, alpha):
    x1 = x1_ref[...]
    x1b = x1.astype(BF16)
    hidden = jnp.square(jnp.maximum(_dot(x1b, wup_ref[...]) + bup_ref[...], 0.0))
    ff = _dot(hidden.astype(BF16), wdn_ref[...]) + bdn_ref[...]
    ple = _sigmoid(_dot(x1b, wpg_ref[...]) + bpg_ref[...]) * _dot(p_ref[...].astype(BF16), wple_ref[...])
    x2_ref[...] = _layer_norm(alpha * x1 + ff + ple, g2_ref[...], b2_ref[...])


def _ffn(x1, p2d, w_up, b_up, w_down, b_down, w_ple, w_pg, b_pg, g2, b2, alpha, tm):
    m, d = x1.shape
    return pl.pallas_call(
        functools.partial(_ffn_kernel, alpha=alpha),
        grid=(m // tm,),
        in_specs=[_row_spec(tm, d), _row_spec(tm, p2d.shape[1]),
                  _const_spec(w_up.shape), _const_spec((1, w_up.shape[1])),
                  _const_spec(w_down.shape), _const_spec((1, d)),
                  _const_spec(w_ple.shape), _const_spec(w_pg.shape), _const_spec((1, d)),
                  _const_spec((1, d)), _const_spec((1, d))],
        out_specs=_row_spec(tm, d),
        out_shape=jax.ShapeDtypeStruct((m, d), F32),
        compiler_params=_params("parallel"),
        name="ffn_ln2",
    )(x1, p2d, w_up, b_up, w_down, b_down, w_ple, w_pg, b_pg, g2, b2)


def _decoder_layer(x, p, h0_re, h0_im, wts, ops, q, alpha, row_tile):
    bn, ln, d = x.shape
    m = bn * ln
    ssm_w = wts["d_skip"].shape[1]
    sgu_w = wts["ln_g"].shape[1]
    x2d = x.reshape(m, d)
    tm = min(row_tile, m)
    u_a, u_b, v_b, gates = _inproj(x2d, wts["w_in"], wts["b_in"], wts["ln_g"], wts["ln_b"], ssm_w, sgu_w, tm)

    nk_total = ln // q
    u4 = jnp.transpose(u_a.reshape(bn, ln, ssm_w), (1, 0, 2)).reshape(nk_total, q, bn, ssm_w)
    y4, h_re, h_im = _s5(u4, h0_re, h0_im, ops, wts["d_skip"], wts["w_glu"], wts["b_glu"],
                         min(S5_CHUNKS_PER_STEP, nk_total))
    y_a = jnp.transpose(y4.reshape(ln, bn, ssm_w), (1, 0, 2)).reshape(m, ssm_w)

    x1 = _mix(u_b, v_b, y_a, gates, x2d, wts["sgu_w"], wts["s_row"], wts["s_bias"], wts["w_pa"], wts["w_pb"],
              wts["w_out"], wts["b_out"], wts["ln1_g"], wts["ln1_b"], alpha, tm, single_position=(ln == 1))
    x2 = _ffn(x1, p.reshape(m, p.shape[-1]), wts["w_up"], wts["b_up"], wts["w_down"], wts["b_down"],
              wts["w_ple"], wts["w_pg"], wts["b_pg"], wts["ln2_g"], wts["ln2_b"], alpha, tm)
    return x2.reshape(bn, ln, d), h_re, h_im, v_b.reshape(bn, ln, sgu_w)


def kernel(x_prompt, x_sample, state_ssm_re, state_ssm_im, p_prompt, p_sample, w_in, b_in, ssm_lambda_re, ssm_lambda_im, ssm_log_dt, ssm_b_re, ssm_b_im, ssm_c_re, ssm_c_im, ssm_d, w_glu, b_glu, sgu_ln_g, sgu_ln_b, sgu_w, sgu_b, w_branch_a, w_branch_b, w_out, b_out, ln1_g, ln1_b, w_up, b_up, w_down, b_down, w_ple, w_ple_gate, b_ple_gate, ln2_g, ln2_b):
    depth = w_in.shape[0]
    g, p_state = ssm_lambda_re.shape[1:]
    heads, chunk = sgu_b.shape[1:]
    sgu_width = sgu_ln_g.shape[1]
    hd = sgu_width // heads
    alpha = float((2 * depth) ** 0.25)
    row = lambda a: a.reshape(1, -1)

    xp, xs = x_prompt, x_sample
    bp, lp = x_prompt.shape[:2]
    bs, ls = x_sample.shape[:2]
    q_prompt = math.gcd(lp, S5_CHUNK)
    q_sample = math.gcd(ls, S5_CHUNK)
    zeros_p = jnp.zeros((bp, g * p_state), F32)
    outs = {k: [] for k in ("pre", "pim", "sre", "sim", "sv")}
    for i in range(depth):
        disc = _discretise(ssm_lambda_re[i], ssm_lambda_im[i], ssm_log_dt[i], ssm_b_re[i], ssm_b_im[i],
                           ssm_c_re[i], ssm_c_im[i], S5_CHUNK)
        ops_prompt = _chunk_operators(*disc, q_prompt)
        ops_sample = ops_prompt if q_sample == q_prompt else _chunk_operators(*disc, q_sample)
        wts = dict(
            w_in=w_in[i].astype(BF16), b_in=row(b_in[i]), ln_g=row(sgu_ln_g[i]), ln_b=row(sgu_ln_b[i]),
            d_skip=row(ssm_d[i]), w_glu=w_glu[i].astype(BF16), b_glu=row(b_glu[i]),
            sgu_w=sgu_w[i],
            s_row=jnp.repeat(sgu_w[i][:, 0, 0], hd).reshape(1, sgu_width),
            s_bias=jnp.repeat(sgu_b[i].T, hd, axis=1),
            w_pa=w_branch_a[i].astype(BF16), w_pb=w_branch_b[i].astype(BF16),
            w_out=w_out[i].astype(BF16), b_out=row(b_out[i]), ln1_g=row(ln1_g[i]), ln1_b=row(ln1_b[i]),
            w_up=w_up[i].astype(BF16), b_up=row(b_up[i]), w_down=w_down[i].astype(BF16), b_down=row(b_down[i]),
            w_ple=w_ple[i].astype(BF16), w_pg=w_ple_gate[i].astype(BF16), b_pg=row(b_ple_gate[i]),
            ln2_g=row(ln2_g[i]), ln2_b=row(ln2_b[i]),
        )
        xp, hpr, hpi, _ = _decoder_layer(xp, p_prompt[i], zeros_p, zeros_p, wts, ops_prompt, q_prompt, alpha,
                                         row_tile=512)
        xs, hsr, hsi, vs = _decoder_layer(xs, p_sample[i], state_ssm_re[i].reshape(bs, g * p_state),
                                          state_ssm_im[i].reshape(bs, g * p_state), wts, ops_sample, q_sample,
                                          alpha, row_tile=512)
        outs["pre"].append(hpr.reshape(bp, g, p_state))
        outs["pim"].append(hpi.reshape(bp, g, p_state))
        outs["sre"].append(hsr.reshape(bs, g, p_state))
        outs["sim"].append(hsi.reshape(bs, g, p_state))
        outs["sv"].append(vs)
    return (xp, xs, jnp.stack(outs["pre"]), jnp.stack(outs["pim"]), jnp.stack(outs["sre"]),
            jnp.stack(outs["sim"]), jnp.stack(outs["sv"]))
```

```python
import functools
import math

import jax
import jax.numpy as jnp
from jax import lax
from jax.experimental import pallas as pl
from jax.experimental.pallas import tpu as pltpu

F32 = jnp.float32
BF16 = jnp.bfloat16

LN_EPS = 1e-5
SGU_CHUNK = 128
GROUPS_PER_BLOCK = 8
S5_CHUNK = 4
S5_CHUNKS_PER_STEP = 64
VMEM_LIMIT_BYTES = 56 * 1024 * 1024


def _gelu(x):
    c = math.sqrt(2.0 / math.pi)
    return x * (0.5 * (1.0 + jnp.tanh(c * (x + 0.044715 * (x * x * x)))))


def _sigmoid(x):
    return 1.0 / (1.0 + jnp.exp(-x))


def _layer_norm(x, g, b):
    mu = jnp.mean(x, axis=-1, keepdims=True)
    xc = x - mu
    var = jnp.mean(xc * xc, axis=-1, keepdims=True)
    return xc * lax.rsqrt(var + LN_EPS) * g + b


def _dot(a, b):
    return jnp.dot(a, b, preferred_element_type=F32)


def _const_spec(shape):
    zeros = (0,) * len(shape)
    return pl.BlockSpec(shape, lambda i: zeros, pipeline_mode=pl.Buffered(1))


def _row_spec(rows, cols):
    return pl.BlockSpec((rows, cols), lambda i: (i, 0))


def _params(semantics):
    return pltpu.CompilerParams(dimension_semantics=(semantics,), vmem_limit_bytes=VMEM_LIMIT_BYTES)


def _lambda_bar(lr, li, ldt):
    dt = jnp.exp(ldt)
    mag = jnp.exp(lr * dt)
    return mag * jnp.cos(li * dt), mag * jnp.sin(li * dt)


def _block_diag(x):
    nl, r, s = x.shape
    flat = x.reshape(nl * r, s)
    tiled = jnp.concatenate([flat] * nl, axis=-1)
    same_group = (lax.broadcasted_iota(jnp.int32, tiled.shape, 0) // r
                  == lax.broadcasted_iota(jnp.int32, tiled.shape, 1) // s)
    return jnp.where(same_group, tiled, 0.0)


def _discretise_kernel(lr2_ref, li2_ref, ldt2_ref, lr3_ref, li3_ref, ldt3_ref, br_ref, bi_ref, cr_ref, ci_ref,
                       a1r_ref, a1i_ref, aqr_ref, aqi_ref, bin_ref, cre_ref, cim_ref, kmix_ref, *, q):
    nb = bin_ref.shape[0]
    gpb = GROUPS_PER_BLOCK
    g, c, p = br_ref.shape
    slab = gpb * c

    ar, ai = _lambda_bar(lr2_ref[...], li2_ref[...], ldt2_ref[...])
    pr, pi = ar, ai
    for _ in range(q - 1):
        pr, pi = ar * pr - ai * pi, ar * pi + ai * pr
    for j in range(nb):
        blk = slice(j * gpb, (j + 1) * gpb)
        for ref, val in ((a1r_ref, ar), (a1i_ref, ai), (aqr_ref, pr), (aqi_ref, pi)):
            ref[j] = jnp.sum(_block_diag(val[blk].reshape(gpb, 1, p)), axis=0, keepdims=True)

    shape3 = (g, c, p)
    lr = jnp.broadcast_to(lr3_ref[...], shape3)
    li = jnp.broadcast_to(li3_ref[...], shape3)
    ar, ai = _lambda_bar(lr, li, jnp.broadcast_to(ldt3_ref[...], shape3))
    den = lr * lr + li * li
    nr = ar - 1.0
    co_re = (nr * lr + ai * li) / den
    co_im = (ai * lr - nr * li) / den
    br = br_ref[...]
    bi = bi_ref[...]
    c_re = cr_ref[...]
    c_im = ci_ref[...]
    c_re_b = c_re.astype(BF16)
    c_im_b = c_im.astype(BF16)
    wr = co_re * br - co_im * bi
    wi = co_re * bi + co_im * br
    pr, pi = ar, ai
    for d in range(q):
        km = (jnp.einsum("gcp,gdp->gcd", c_re_b, wr.astype(BF16), preferred_element_type=F32)
              - jnp.einsum("gcp,gdp->gcd", c_im_b, wi.astype(BF16), preferred_element_type=F32))
        cer = c_re * pr - c_im * pi
        cei = c_re * pi + c_im * pr
        rows_in = slice((q - 1 - d) * slab, (q - d) * slab)
        cols_out = slice(d * slab, (d + 1) * slab)
        for j in range(nb):
            blk = slice(j * gpb, (j + 1) * gpb)
            bin_ref[j, rows_in, :gpb * p] = _block_diag(wr[blk]).astype(BF16)
            bin_ref[j, rows_in, gpb * p:] = _block_diag(wi[blk]).astype(BF16)
            cre_ref[j, :, cols_out] = _block_diag(cer[blk]).T.astype(BF16)
            cim_ref[j, :, cols_out] = _block_diag(cei[blk]).T.astype(BF16)
            kd = _block_diag(km[blk]).T.astype(BF16)
            for ip in range(q - d):
                kmix_ref[j, ip * slab:(ip + 1) * slab, (ip + d) * slab:(ip + d + 1) * slab] = kd
            if d > 0:
                zero = jnp.zeros((slab, slab), BF16)
                for i in range(q - d):
                    kmix_ref[j, (i + d) * slab:(i + d + 1) * slab, i * slab:(i + 1) * slab] = zero
        wr, wi = ar * wr - ai * wi, ar * wi + ai * wr
        pr, pi = ar * pr - ai * pi, ar * pi + ai * pr


def _discretise(lam_re, lam_im, log_dt, b_re, b_im, c_re, c_im, q):
    g, p, c = b_re.shape
    nb = g // GROUPS_PER_BLOCK
    n = GROUPS_PER_BLOCK * p
    slab = GROUPS_PER_BLOCK * c
    t = lambda a: jnp.transpose(a, (0, 2, 1))
    row = jax.ShapeDtypeStruct((nb, 1, n), F32)
    out_shape = (row, row, row, row,
                 jax.ShapeDtypeStruct((nb, q * slab, 2 * n), BF16),
                 jax.ShapeDtypeStruct((nb, n, q * slab), BF16), jax.ShapeDtypeStruct((nb, n, q * slab), BF16),
                 jax.ShapeDtypeStruct((nb, q * slab, q * slab), BF16))
    a1r, a1i, aqr, aqi, b_in, cre, cim, k_mix = pl.pallas_call(
        functools.partial(_discretise_kernel, q=q), out_shape=out_shape, name="s5_discretise",
        compiler_params=pltpu.CompilerParams(vmem_limit_bytes=VMEM_LIMIT_BYTES),
    )(lam_re, lam_im, log_dt.reshape(g, 1), lam_re.reshape(g, 1, p), lam_im.reshape(g, 1, p),
      log_dt.reshape(g, 1, 1), t(b_re), t(b_im), c_re, c_im)
    return dict(a1_re=a1r, a1_im=a1i, aq_re=aqr, aq_im=aqi, b_in=b_in, c_re=cre, c_im=cim, k_mix=k_mix)


def _inproj_kernel(x_ref, w_ref, b_ref, lng_ref, lnb_ref, ua_ref, ub_ref, vb_ref, gates_ref, *, ssm_w, sgu_w):
    xb = x_ref[...].astype(BF16)

    def seg(lo, hi):
        return _dot(xb, w_ref[:, lo:hi]) + b_ref[:, lo:hi]

    ua_ref[...] = seg(0, ssm_w)
    ub_ref[...] = _gelu(seg(ssm_w, ssm_w + sgu_w)).astype(BF16)
    v = _gelu(seg(ssm_w + sgu_w, ssm_w + 2 * sgu_w))
    vb_ref[...] = _layer_norm(v, lng_ref[...], lnb_ref[...])
    gates_ref[...] = _sigmoid(seg(ssm_w + 2 * sgu_w, w_ref.shape[1])).astype(BF16)


def _inproj(x2d, w_in, b_in, ln_g, ln_b, ssm_w, sgu_w, tm):
    m, d = x2d.shape
    cols = w_in.shape[1]
    gate_w = cols - ssm_w - 2 * sgu_w
    out_shape = (jax.ShapeDtypeStruct((m, ssm_w), F32), jax.ShapeDtypeStruct((m, sgu_w), BF16),
                 jax.ShapeDtypeStruct((m, sgu_w), F32), jax.ShapeDtypeStruct((m, gate_w), BF16))
    return pl.pallas_call(
        functools.partial(_inproj_kernel, ssm_w=ssm_w, sgu_w=sgu_w),
        grid=(m // tm,),
        in_specs=[_row_spec(tm, d), _const_spec((d, cols)), _const_spec((1, cols)),
                  _const_spec((1, sgu_w)), _const_spec((1, sgu_w))],
        out_specs=(_row_spec(tm, ssm_w), _row_spec(tm, sgu_w), _row_spec(tm, sgu_w), _row_spec(tm, gate_w)),
        out_shape=out_shape,
        compiler_params=_params("parallel"),
        name="in_proj",
    )(x2d, w_in, b_in, ln_g, ln_b)


def _s5_kernel(u_ref, h0r_ref, h0i_ref, ar_ref, ai_ref, bin_ref, cre_ref, cim_ref, kmix_ref, d_ref, wglu_ref,
               bglu_ref, ya_ref, hr_ref, hi_ref, v_ref, y_ref):
    @pl.when(pl.program_id(0) == 0)
    def _():
        hr_ref[...] = h0r_ref[...]
        hi_ref[...] = h0i_ref[...]

    nk, q, bsub, w = u_ref.shape
    n_blocks, _, two_n = bin_ref.shape
    n = two_n // 2
    slab = w // n_blocks
    rows = nk * bsub
    for j in range(n_blocks):
        cols = slice(j * n, (j + 1) * n)
        lanes = slice(j * slab, (j + 1) * slab)
        a_re = jnp.broadcast_to(ar_ref[j], (bsub, n))
        a_im = jnp.broadcast_to(ai_ref[j], (bsub, n))
        xcat = jnp.concatenate([u_ref[:, i, :, lanes].reshape(rows, slab) for i in range(q)],
                               axis=-1).astype(BF16)
        v_ref[...] = _dot(xcat, bin_ref[j])
        h_re = hr_ref[:, cols]
        h_im = hi_ref[:, cols]
        if nk == 1:
            n_re = a_re * h_re - a_im * h_im + v_ref[:, :n]
            n_im = a_re * h_im + a_im * h_re + v_ref[:, n:]
            v_ref[:, :n] = h_re
            v_ref[:, n:] = h_im
            h_re, h_im = n_re, n_im
        else:
            def body(k, carry):
                c_re, c_im = carry
                r = pl.ds(pl.multiple_of(k * bsub, bsub), bsub)
                n_re = a_re * c_re - a_im * c_im + v_ref[r, :n]
                n_im = a_re * c_im + a_im * c_re + v_ref[r, n:]
                v_ref[r, :n] = c_re
                v_ref[r, n:] = c_im
                return n_re, n_im

            h_re, h_im = lax.fori_loop(0, nk, body, (h_re, h_im), unroll=8)
        hr_ref[:, cols] = h_re
        hi_ref[:, cols] = h_im
        y = (_dot(v_ref[:, :n].astype(BF16), cre_ref[j]) - _dot(v_ref[:, n:].astype(BF16), cim_ref[j])
             + _dot(xcat, kmix_ref[j]))
        for i in range(q):
            y_ref[:, i, :, lanes] = y[:, i * slab:(i + 1) * slab].reshape(nk, bsub, slab)
    m = nk * q * bsub
    y = _gelu(y_ref[...].reshape(m, w) + d_ref[...] * u_ref[...].reshape(m, w))
    ya = y * _sigmoid(_dot(y.astype(BF16), wglu_ref[...]) + bglu_ref[...])
    ya_ref[...] = ya.reshape(nk, q, bsub, w).astype(BF16)


def _s5(u4, h0_re, h0_im, ops, d_skip, w_glu, b_glu, nk):
    nk_total, q, bsub, w = u4.shape
    n_blocks, q_slab, two_n = ops["b_in"].shape
    n = two_n // 2
    gp = n_blocks * n
    slab = w // n_blocks
    q_ops = q_slab // slab
    assert q in (1, q_ops), (q, q_ops)
    a_re, a_im = (ops["aq_re"], ops["aq_im"]) if q == q_ops else (ops["a1_re"], ops["a1_im"])

    def sub_block(shape, index):
        return pl.BlockSpec(shape, lambda i: index, pipeline_mode=pl.Buffered(1))

    blk = pl.BlockSpec((nk, q, bsub, w), lambda i: (i, 0, 0, 0))
    state_spec = pl.BlockSpec((bsub, gp), lambda i: (0, 0))
    out_shape = (jax.ShapeDtypeStruct(u4.shape, BF16),
                 jax.ShapeDtypeStruct((bsub, gp), F32), jax.ShapeDtypeStruct((bsub, gp), F32))
    return pl.pallas_call(
        _s5_kernel,
        grid=(nk_total // nk,),
        in_specs=[blk, _const_spec((bsub, gp)), _const_spec((bsub, gp)),
                  _const_spec(a_re.shape), _const_spec(a_im.shape),
                  sub_block((n_blocks, q * slab, two_n), (0, q_ops // q - 1, 0)),
                  sub_block((n_blocks, n, q * slab), (0, 0, 0)), sub_block((n_blocks, n, q * slab), (0, 0, 0)),
                  sub_block((n_blocks, q * slab, q * slab), (0, 0, 0)),
                  _const_spec((1, w)), _const_spec((w, w)), _const_spec((1, w))],
        out_specs=(blk, state_spec, state_spec),
        out_shape=out_shape,
        scratch_shapes=[pltpu.VMEM((nk * bsub, two_n), F32), pltpu.VMEM((nk, q, bsub, w), F32)],
        compiler_params=_params("arbitrary"),
        name="s5_scan",
    )(u4, h0_re, h0_im, a_re, a_im, ops["b_in"], ops["c_re"], ops["c_im"], ops["k_mix"], d_skip, w_glu, b_glu)


def _mix_kernel(ub_ref, vb_ref, ya_ref, gates_ref, x_ref, ws_ref, srow_ref, sbias_ref, wpa_ref, wpb_ref,
                wout_ref, bout_ref, g1_ref, b1_ref, x1_ref, yb_ref, *, alpha, single_position):
    tm, sgu_w = ub_ref.shape
    heads, chunk, _ = ws_ref.shape
    hd = sgu_w // heads
    if single_position:
        s = vb_ref[...] * srow_ref[...] + sbias_ref[0:1, :]
        yb_ref[...] = (ub_ref[...].astype(F32) * s).astype(BF16)
    else:
        keep = (lax.broadcasted_iota(jnp.int32, (chunk, chunk), 0)
                >= lax.broadcasted_iota(jnp.int32, (chunk, chunk), 1))
        for h in range(heads):
            wm = jnp.where(keep, ws_ref[h], 0.0).astype(BF16)
            hc = slice(h * hd, (h + 1) * hd)
            for c in range(tm // chunk):
                rc = slice(c * chunk, (c + 1) * chunk)
                s = _dot(wm, vb_ref[rc, hc].astype(BF16)) + sbias_ref[:, hc]
                yb_ref[rc, hc] = (ub_ref[rc, hc].astype(F32) * s).astype(BF16)
    d = x_ref.shape[1]
    ga = gates_ref[:, :d].astype(F32)
    gb = gates_ref[:, d:].astype(F32)
    merged = ga * _dot(ya_ref[...], wpa_ref[...]) + gb * _dot(yb_ref[...], wpb_ref[...])
    mix = _dot(merged.astype(BF16), wout_ref[...]) + bout_ref[...]
    x1_ref[...] = _layer_norm(alpha * x_ref[...] + mix, g1_ref[...], b1_ref[...])


def _mix(u_b, v_b, y_a, gates, x2d, sgu_w_mat, s_row, s_bias, w_pa, w_pb, w_out, b_out, g1, b1, alpha, tm,
         single_position):
    m, d = x2d.shape
    sgu_w = u_b.shape[1]
    return pl.pallas_call(
        functools.partial(_mix_kernel, alpha=alpha, single_position=single_position),
        grid=(m // tm,),
        in_specs=[_row_spec(tm, sgu_w), _row_spec(tm, sgu_w), _row_spec(tm, y_a.shape[1]),
                  _row_spec(tm, gates.shape[1]), _row_spec(tm, d),
                  _const_spec(sgu_w_mat.shape), _const_spec(s_row.shape), _const_spec(s_bias.shape),
                  _const_spec(w_pa.shape), _const_spec(w_pb.shape), _const_spec(w_out.shape),
                  _const_spec((1, d)), _const_spec((1, d)), _const_spec((1, d))],
        out_specs=_row_spec(tm, d),
        out_shape=jax.ShapeDtypeStruct((m, d), F32),
        scratch_shapes=[pltpu.VMEM((tm, sgu_w), BF16)],
        compiler_params=_params("parallel"),
        name="mix_merge_ln1",
    )(u_b, v_b, y_a, gates, x2d, sgu_w_mat, s_row, s_bias, w_pa, w_pb, w_out, b_out, g1, b1)


def _ffn_kernel(x1_ref, p_ref, wup_ref, bup_ref, wdn_ref, bdn_ref, wple_ref, wpg_ref, bpg_ref, g2_ref, b2_ref,
                x2_ref, *, alpha):
    x1 = x1_ref[...]
    x1b = x1.astype(BF16)
    hidden = jnp.square(jnp.maximum(_dot(x1b, wup_ref[...]) + bup_ref[...], 0.0))
    ff = _dot(hidden.astype(BF16), wdn_ref[...]) + bdn_ref[...]
    ple = _sigmoid(_dot(x1b, wpg_ref[...]) + bpg_ref[...]) * _dot(p_ref[...].astype(BF16), wple_ref[...])
    x2_ref[...] = _layer_norm(alpha * x1 + ff + ple, g2_ref[...], b2_ref[...])


def _ffn(x1, p2d, w_up, b_up, w_down, b_down, w_ple, w_pg, b_pg, g2, b2, alpha, tm):
    m, d = x1.shape
    return pl.pallas_call(
        functools.partial(_ffn_kernel, alpha=alpha),
        grid=(m // tm,),
        in_specs=[_row_spec(tm, d), _row_spec(tm, p2d.shape[1]),
                  _const_spec(w_up.shape), _const_spec((1, w_up.shape[1])),
                  _const_spec(w_down.shape), _const_spec((1, d)),
                  _const_spec(w_ple.shape), _const_spec(w_pg.shape), _const_spec((1, d)),
                  _const_spec((1, d)), _const_spec((1, d))],
        out_specs=_row_spec(tm, d),
        out_shape=jax.ShapeDtypeStruct((m, d), F32),
        compiler_params=_params("parallel"),
        name="ffn_ln2",
    )(x1, p2d, w_up, b_up, w_down, b_down, w_ple, w_pg, b_pg, g2, b2)


def _decoder_layer(x, p, h0_re, h0_im, wts, ops, q, alpha, row_tile):
    bn, ln, d = x.shape
    m = bn * ln
    ssm_w = wts["d_skip"].shape[1]
    sgu_w = wts["ln_g"].shape[1]
    x2d = x.reshape(m, d)
    tm = min(row_tile, m)
    u_a, u_b, v_b, gates = _inproj(x2d, wts["w_in"], wts["b_in"], wts["ln_g"], wts["ln_b"], ssm_w, sgu_w, tm)

    nk_total = ln // q
    u4 = jnp.transpose(u_a.reshape(bn, ln, ssm_w), (1, 0, 2)).reshape(nk_total, q, bn, ssm_w)
    y4, h_re, h_im = _s5(u4, h0_re, h0_im, ops, wts["d_skip"], wts["w_glu"], wts["b_glu"],
                         min(S5_CHUNKS_PER_STEP, nk_total))
    y_a = jnp.transpose(y4.reshape(ln, bn, ssm_w), (1, 0, 2)).reshape(m, ssm_w)

    x1 = _mix(u_b, v_b, y_a, gates, x2d, wts["sgu_w"], wts["s_row"], wts["s_bias"], wts["w_pa"], wts["w_pb"],
              wts["w_out"], wts["b_out"], wts["ln1_g"], wts["ln1_b"], alpha, tm, single_position=(ln == 1))
    x2 = _ffn(x1, p.reshape(m, p.shape[-1]), wts["w_up"], wts["b_up"], wts["w_down"], wts["b_down"],
              wts["w_ple"], wts["w_pg"], wts["b_pg"], wts["ln2_g"], wts["ln2_b"], alpha, tm)
    return x2.reshape(bn, ln, d), h_re, h_im, v_b.reshape(bn, ln, sgu_w)


def kernel(x_prompt, x_sample, state_ssm_re, state_ssm_im, p_prompt, p_sample, w_in, b_in, ssm_lambda_re, ssm_lambda_im, ssm_log_dt, ssm_b_re, ssm_b_im, ssm_c_re, ssm_c_im, ssm_d, w_glu, b_glu, sgu_ln_g, sgu_ln_b, sgu_w, sgu_b, w_branch_a, w_branch_b, w_out, b_out, ln1_g, ln1_b, w_up, b_up, w_down, b_down, w_ple, w_ple_gate, b_ple_gate, ln2_g, ln2_b):
    depth = w_in.shape[0]
    g, p_state = ssm_lambda_re.shape[1:]
    heads, chunk = sgu_b.shape[1:]
    sgu_width = sgu_ln_g.shape[1]
    hd = sgu_width // heads
    alpha = float((2 * depth) ** 0.25)
    row = lambda a: a.reshape(1, -1)

    xp, xs = x_prompt, x_sample
    bp, lp = x_prompt.shape[:2]
    bs, ls = x_sample.shape[:2]
    q_prompt = math.gcd(lp, S5_CHUNK)
    q_sample = math.gcd(ls, S5_CHUNK)
    zeros_p = jnp.zeros((bp, g * p_state), F32)
    outs = {k: [] for k in ("pre", "pim", "sre", "sim", "sv")}
    for i in range(depth):
        ops = _discretise(ssm_lambda_re[i], ssm_lambda_im[i], ssm_log_dt[i], ssm_b_re[i], ssm_b_im[i],
                          ssm_c_re[i], ssm_c_im[i], S5_CHUNK)
        wts = dict(
            w_in=w_in[i].astype(BF16), b_in=row(b_in[i]), ln_g=row(sgu_ln_g[i]), ln_b=row(sgu_ln_b[i]),
            d_skip=row(ssm_d[i]), w_glu=w_glu[i].astype(BF16), b_glu=row(b_glu[i]),
            sgu_w=sgu_w[i],
            s_row=jnp.repeat(sgu_w[i][:, 0, 0], hd).reshape(1, sgu_width),
            s_bias=jnp.repeat(sgu_b[i].T, hd, axis=1),
            w_pa=w_branch_a[i].astype(BF16), w_pb=w_branch_b[i].astype(BF16),
            w_out=w_out[i].astype(BF16), b_out=row(b_out[i]), ln1_g=row(ln1_g[i]), ln1_b=row(ln1_b[i]),
            w_up=w_up[i].astype(BF16), b_up=row(b_up[i]), w_down=w_down[i].astype(BF16), b_down=row(b_down[i]),
            w_ple=w_ple[i].astype(BF16), w_pg=w_ple_gate[i].astype(BF16), b_pg=row(b_ple_gate[i]),
            ln2_g=row(ln2_g[i]), ln2_b=row(ln2_b[i]),
        )
        xp, hpr, hpi, _ = _decoder_layer(xp, p_prompt[i], zeros_p, zeros_p, wts, ops, q_prompt, alpha,
                                         row_tile=512)
        xs, hsr, hsi, vs = _decoder_layer(xs, p_sample[i], state_ssm_re[i].reshape(bs, g * p_state),
                                          state_ssm_im[i].reshape(bs, g * p_state), wts, ops, q_sample,
                                          alpha, row_tile=512)
        outs["pre"].append(hpr.reshape(bp, g, p_state))
        outs["pim"].append(hpi.reshape(bp, g, p_state))
        outs["sre"].append(hsr.reshape(bs, g, p_state))
        outs["sim"].append(hsi.reshape(bs, g, p_state))
        outs["sv"].append(vs)
    return (xp, xs, jnp.stack(outs["pre"]), jnp.stack(outs["pim"]), jnp.stack(outs["sre"]),
            jnp.stack(outs["sim"]), jnp.stack(outs["sv"]))
```

```python
import functools
import math

import jax
import jax.numpy as jnp
from jax import lax
from jax.experimental import pallas as pl
from jax.experimental.pallas import tpu as pltpu

F32 = jnp.float32
BF16 = jnp.bfloat16

LN_EPS = 1e-5
LANES = 128
SUBLANES = 8
GROUPS_PER_BLOCK = 8
S5_CHUNK = 4
FFN_ROW_TILE = 512
VMEM_LIMIT_BYTES = 58 * 1024 * 1024


def _gelu(x):
    c = math.sqrt(2.0 / math.pi)
    return x * (0.5 * (1.0 + jnp.tanh(c * (x + 0.044715 * (x * x * x)))))


def _sigmoid(x):
    return 1.0 / (1.0 + jnp.exp(-x))


def _layer_norm(x, g, b):
    mu = jnp.mean(x, axis=-1, keepdims=True)
    xc = x - mu
    var = jnp.mean(xc * xc, axis=-1, keepdims=True)
    return xc * lax.rsqrt(var + LN_EPS) * g + b


def _dot(a, b):
    return jnp.dot(a, b, preferred_element_type=F32)


def _resident(shape, index=None):
    index = (0,) * len(shape) if index is None else index
    return pl.BlockSpec(shape, lambda i: index, pipeline_mode=pl.Buffered(1))


def _row_spec(rows, cols):
    return pl.BlockSpec((rows, cols), lambda i: (i, 0))


def _params(semantics):
    return pltpu.CompilerParams(dimension_semantics=(semantics,), vmem_limit_bytes=VMEM_LIMIT_BYTES)


def _lambda_bar(lr, li, ldt):
    dt = jnp.exp(ldt)
    mag = jnp.exp(lr * dt)
    return mag * jnp.cos(li * dt), mag * jnp.sin(li * dt)


def _block_diag(x):
    nl, r, s = x.shape
    flat = x.reshape(nl * r, s)
    tiled = jnp.concatenate([flat] * nl, axis=-1)
    same_group = (lax.broadcasted_iota(jnp.int32, tiled.shape, 0) // r
                  == lax.broadcasted_iota(jnp.int32, tiled.shape, 1) // s)
    return jnp.where(same_group, tiled, 0.0)


def _discretise_kernel(lr2_ref, li2_ref, ldt2_ref, lr3_ref, li3_ref, ldt3_ref, br_ref, bi_ref, cr_ref, ci_ref,
                       a1r_ref, a1i_ref, aqr_ref, aqi_ref, bin_ref, cre_ref, cim_ref, kmix_ref, *, q):
    nb = bin_ref.shape[0]
    gpb = GROUPS_PER_BLOCK
    g, c, p = br_ref.shape
    slab = gpb * c

    ar, ai = _lambda_bar(lr2_ref[...], li2_ref[...], ldt2_ref[...])
    pr, pi = ar, ai
    for _ in range(q - 1):
        pr, pi = ar * pr - ai * pi, ar * pi + ai * pr
    for j in range(nb):
        blk = slice(j * gpb, (j + 1) * gpb)
        for ref, val in ((a1r_ref, ar), (a1i_ref, ai), (aqr_ref, pr), (aqi_ref, pi)):
            ref[j] = jnp.sum(_block_diag(val[blk].reshape(gpb, 1, p)), axis=0, keepdims=True)

    shape3 = (g, c, p)
    lr = jnp.broadcast_to(lr3_ref[...], shape3)
    li = jnp.broadcast_to(li3_ref[...], shape3)
    ar, ai = _lambda_bar(lr, li, jnp.broadcast_to(ldt3_ref[...], shape3))
    den = lr * lr + li * li
    nr = ar - 1.0
    co_re = (nr * lr + ai * li) / den
    co_im = (ai * lr - nr * li) / den
    br = br_ref[...]
    bi = bi_ref[...]
    c_re = cr_ref[...]
    c_im = ci_ref[...]
    c_re_b = c_re.astype(BF16)
    c_im_b = c_im.astype(BF16)
    wr = co_re * br - co_im * bi
    wi = co_re * bi + co_im * br
    pr, pi = ar, ai
    for d in range(q):
        km = (jnp.einsum("gcp,gdp->gcd", c_re_b, wr.astype(BF16), preferred_element_type=F32)
              - jnp.einsum("gcp,gdp->gcd", c_im_b, wi.astype(BF16), preferred_element_type=F32))
        cer = c_re * pr - c_im * pi
        cei = c_re * pi + c_im * pr
        rows_in = slice((q - 1 - d) * slab, (q - d) * slab)
        cols_out = slice(d * slab, (d + 1) * slab)
        for j in range(nb):
            blk = slice(j * gpb, (j + 1) * gpb)
            bin_ref[j, rows_in, :gpb * p] = _block_diag(wr[blk]).astype(BF16)
            bin_ref[j, rows_in, gpb * p:] = _block_diag(wi[blk]).astype(BF16)
            cre_ref[j, :, cols_out] = _block_diag(cer[blk]).T.astype(BF16)
            cim_ref[j, :, cols_out] = _block_diag(cei[blk]).T.astype(BF16)
            kd = _block_diag(km[blk]).T.astype(BF16)
            for ip in range(q - d):
                kmix_ref[j, ip * slab:(ip + 1) * slab, (ip + d) * slab:(ip + d + 1) * slab] = kd
            if d > 0:
                zero = jnp.zeros((slab, slab), BF16)
                for i in range(q - d):
                    kmix_ref[j, (i + d) * slab:(i + d + 1) * slab, i * slab:(i + 1) * slab] = zero
        wr, wi = ar * wr - ai * wi, ar * wi + ai * wr
        pr, pi = ar * pr - ai * pi, ar * pi + ai * pr


def _discretise(lam_re, lam_im, log_dt, b_re, b_im, c_re, c_im, q):
    g, p, c = b_re.shape
    nb = g // GROUPS_PER_BLOCK
    n = GROUPS_PER_BLOCK * p
    slab = GROUPS_PER_BLOCK * c
    t = lambda a: jnp.transpose(a, (0, 2, 1))
    row = jax.ShapeDtypeStruct((nb, 1, n), F32)
    out_shape = (row, row, row, row,
                 jax.ShapeDtypeStruct((nb, q * slab, 2 * n), BF16),
                 jax.ShapeDtypeStruct((nb, n, q * slab), BF16), jax.ShapeDtypeStruct((nb, n, q * slab), BF16),
                 jax.ShapeDtypeStruct((nb, q * slab, q * slab), BF16))
    a1r, a1i, aqr, aqi, b_in, cre, cim, k_mix = pl.pallas_call(
        functools.partial(_discretise_kernel, q=q), out_shape=out_shape, name="s5_discretise",
        compiler_params=pltpu.CompilerParams(vmem_limit_bytes=VMEM_LIMIT_BYTES),
    )(lam_re, lam_im, log_dt.reshape(g, 1), lam_re.reshape(g, 1, p), lam_im.reshape(g, 1, p),
      log_dt.reshape(g, 1, 1), t(b_re), t(b_im), c_re, c_im)
    return dict(a1_re=a1r, a1_im=a1i, aq_re=aqr, aq_im=aqi, b_in=b_in, c_re=cre, c_im=cim, k_mix=k_mix)


def _scan_pitch(nk):
    pitch = nk + SUBLANES
    if (pitch // SUBLANES) % 2 == 0:
        pitch += SUBLANES
    return pitch


def _s5_chunked(j, ua_ref, y_ref, v_ref, a_re, a_im, h_re, h_im, sbin_ref, cre_ref, cim_ref, kmix_ref,
                *, q, nseq, npos):
    nk = npos // q
    pitch = _scan_pitch(nk)
    m = nseq * nk
    n = a_re.shape[1]
    ns = n // LANES
    slab = ua_ref.shape[2]
    xcat = jnp.concatenate([ua_ref[j, pl.ds(i, m, stride=q), :] for i in range(q)], axis=-1).astype(BF16)
    v = _dot(xcat, sbin_ref[j])
    for s in range(2 * ns):
        for b in range(nseq):
            v_ref[s, b * pitch:b * pitch + nk, :] = v[b * nk:(b + 1) * nk, s * LANES:(s + 1) * LANES]

    def body(k, carry):
        c_re, c_im = carry
        rows_k = pl.ds(k, nseq, stride=pitch)
        v_re = jnp.concatenate([v_ref[s, rows_k, :] for s in range(ns)], axis=-1)
        v_im = jnp.concatenate([v_ref[ns + s, rows_k, :] for s in range(ns)], axis=-1)
        n_re = a_re * c_re - a_im * c_im + v_re
        n_im = a_re * c_im + a_im * c_re + v_im
        for s in range(ns):
            v_ref[s, rows_k, :] = c_re[:, s * LANES:(s + 1) * LANES]
            v_ref[ns + s, rows_k, :] = c_im[:, s * LANES:(s + 1) * LANES]
        return n_re, n_im

    h_re, h_im = lax.fori_loop(0, nk, body, (h_re, h_im), unroll=4)
    hs_re = jnp.concatenate([v_ref[s] for s in range(ns)], axis=-1).astype(BF16)
    hs_im = jnp.concatenate([v_ref[ns + s] for s in range(ns)], axis=-1).astype(BF16)
    y_state = _dot(hs_re, cre_ref[j]) - _dot(hs_im, cim_ref[j])
    y_input = _dot(xcat, kmix_ref[j])
    for b in range(nseq):
        piece = y_state[b * pitch:b * pitch + nk] + y_input[b * nk:(b + 1) * nk]
        for i in range(q):
            y_ref[j, pl.ds(b * npos + i, nk, stride=q), :] = piece[:, i * slab:(i + 1) * slab]
    return h_re, h_im


def _s5_single(j, ua_ref, y_ref, a_re, a_im, h_re, h_im, sbin_ref, cre_ref, cim_ref, kmix_ref):
    xb = ua_ref[j].astype(BF16)
    v = _dot(xb, sbin_ref[j])
    n = a_re.shape[1]
    y_ref[j] = (_dot(h_re.astype(BF16), cre_ref[j]) - _dot(h_im.astype(BF16), cim_ref[j])
                + _dot(xb, kmix_ref[j]))
    return a_re * h_re - a_im * h_im + v[:, :n], a_re * h_im + a_im * h_re + v[:, n:]


def _mixer_kernel(x_ref, h0r_ref, h0i_ref, win_ref, bias_ref, lng_ref, lnb_ref, ar_ref, ai_ref, sbin_ref,
                  cre_ref, cim_ref, kmix_ref, d_ref, wglu_ref, bglu_ref, ws_ref, srow_ref, sbias_ref,
                  wpa_ref, wpb_ref, *rest, q, nseq, npos, single_position):
    if single_position:
        merged_ref, hr_ref, hi_ref, vb_ref, ua_ref, y_ref, yb_ref = rest
        v_ref = None
    else:
        merged_ref, hr_ref, hi_ref, ua_ref, y_ref, yb_ref, v_ref = rest

    @pl.when(pl.program_id(0) == 0)
    def _():
        hr_ref[...] = h0r_ref[...]
        hi_ref[...] = h0i_ref[...]
        if v_ref is not None:
            v_ref[...] = jnp.zeros(v_ref.shape, F32)

    rows = nseq * npos
    d_model = win_ref.shape[0]
    ssm_w = d_ref.shape[1]
    sgu_w = lng_ref.shape[1]
    n_blocks, slab = ua_ref.shape[0], ua_ref.shape[2]
    n = ar_ref.shape[2]
    xb = x_ref[...].reshape(rows, d_model).astype(BF16)

    def seg(lo, hi):
        return _dot(xb, win_ref[:, lo:hi]) + bias_ref[:, lo:hi]

    ua = seg(0, ssm_w)
    for j in range(n_blocks):
        ua_ref[j] = ua[:, j * slab:(j + 1) * slab]
    for j in range(n_blocks):
        cols = slice(j * n, (j + 1) * n)
        a_re = jnp.broadcast_to(ar_ref[j], (nseq, n))
        a_im = jnp.broadcast_to(ai_ref[j], (nseq, n))
        if single_position:
            h_re, h_im = _s5_single(j, ua_ref, y_ref, a_re, a_im, hr_ref[:, cols], hi_ref[:, cols],
                                    sbin_ref, cre_ref, cim_ref, kmix_ref)
        else:
            h_re, h_im = _s5_chunked(j, ua_ref, y_ref, v_ref, a_re, a_im, hr_ref[:, cols], hi_ref[:, cols],
                                     sbin_ref, cre_ref, cim_ref, kmix_ref, q=q, nseq=nseq, npos=npos)
        hr_ref[:, cols] = h_re
        hi_ref[:, cols] = h_im
    y = jnp.concatenate([y_ref[j] for j in range(n_blocks)], axis=-1)
    u = jnp.concatenate([ua_ref[j] for j in range(n_blocks)], axis=-1)
    y = _gelu(y + d_ref[...] * u)
    ya = (y * _sigmoid(_dot(y.astype(BF16), wglu_ref[...]) + bglu_ref[...])).astype(BF16)

    ub = _gelu(seg(ssm_w, ssm_w + sgu_w))
    v = _layer_norm(_gelu(seg(ssm_w + sgu_w, ssm_w + 2 * sgu_w)), lng_ref[...], lnb_ref[...])
    if single_position:
        vb_ref[...] = v
        yb_ref[...] = (ub * (v * srow_ref[...] + sbias_ref[0:1, :])).astype(BF16)
    else:
        heads, chunk, _ = ws_ref.shape
        hd = sgu_w // heads
        vb = v.astype(BF16)
        keep = (lax.broadcasted_iota(jnp.int32, (chunk, chunk), 0)
                >= lax.broadcasted_iota(jnp.int32, (chunk, chunk), 1))
        for h in range(heads):
            wm = jnp.where(keep, ws_ref[h], 0.0).astype(BF16)
            hc = slice(h * hd, (h + 1) * hd)
            for c in range(rows // chunk):
                rc = slice(c * chunk, (c + 1) * chunk)
                s = _dot(wm, vb[rc, hc]) + sbias_ref[:, hc]
                yb_ref[rc, hc] = (ub[rc, hc] * s).astype(BF16)

    g0 = ssm_w + 2 * sgu_w
    merged = _sigmoid(seg(g0, g0 + d_model)) * _dot(ya, wpa_ref[...])
    merged = merged + _sigmoid(seg(g0 + d_model, g0 + 2 * d_model)) * _dot(yb_ref[...], wpb_ref[...])
    merged_ref[...] = merged.astype(BF16).reshape(merged_ref.shape)


def _mixer(x, h0_re, h0_im, wts, ops, q, npos):
    bn, ln, d = x.shape
    single_position = ln == 1
    n_blocks, q_slab, two_n = ops["b_in"].shape
    n = two_n // 2
    gp = n_blocks * n
    ssm_w = wts["d_skip"].shape[1]
    sgu_w = wts["ln_g"].shape[1]
    slab = ssm_w // n_blocks
    q_ops = q_slab // slab
    assert q in (1, q_ops) and ln % npos == 0 and npos % q == 0, (q, q_ops, ln, npos)
    a_re, a_im = (ops["aq_re"], ops["aq_im"]) if q == q_ops else (ops["a1_re"], ops["a1_im"])
    if single_position:
        x_in = x.reshape(bn, d)
        x_spec = pl.BlockSpec((bn, d), lambda i: (0, 0))
        merged_shape = jax.ShapeDtypeStruct((bn, d), BF16)
        extra_out = [jax.ShapeDtypeStruct((bn, sgu_w), F32)]
        extra_spec = [pl.BlockSpec((bn, sgu_w), lambda i: (0, 0))]
        rows = bn
        scan_scratch = []
    else:
        x_in = x
        x_spec = pl.BlockSpec((bn, npos, d), lambda i: (0, i, 0))
        merged_shape = jax.ShapeDtypeStruct((bn, ln, d), BF16)
        extra_out, extra_spec = [], []
        rows = bn * npos
        scan_scratch = [pltpu.VMEM((two_n // LANES, bn * _scan_pitch(npos // q), LANES), F32)]
    state_spec = pl.BlockSpec((bn, gp), lambda i: (0, 0))
    state_shape = jax.ShapeDtypeStruct((bn, gp), F32)
    res = lambda name: _resident(wts[name].shape)
    outs = pl.pallas_call(
        functools.partial(_mixer_kernel, q=q, nseq=bn, npos=npos, single_position=single_position),
        grid=(ln // npos,),
        in_specs=[x_spec, _resident((bn, gp)), _resident((bn, gp)),
                  res("w_in"), res("b_in"), res("ln_g"), res("ln_b"),
                  _resident(a_re.shape), _resident(a_im.shape),
                  _resident((n_blocks, q * slab, two_n), (0, q_ops // q - 1, 0)),
                  _resident((n_blocks, n, q * slab)), _resident((n_blocks, n, q * slab)),
                  _resident((n_blocks, q * slab, q * slab)),
                  res("d_skip"), res("w_glu"), res("b_glu"), res("sgu_w"), res("s_row"), res("s_bias"),
                  res("w_pa"), res("w_pb")],
        out_specs=[x_spec, state_spec, state_spec] + extra_spec,
        out_shape=[merged_shape, state_shape, state_shape] + extra_out,
        scratch_shapes=[pltpu.VMEM((n_blocks, rows, slab), F32), pltpu.VMEM((n_blocks, rows, slab), F32),
                        pltpu.VMEM((rows, sgu_w), BF16)] + scan_scratch,
        compiler_params=_params("arbitrary"),
        name="mixer",
    )(x_in, h0_re, h0_im, wts["w_in"], wts["b_in"], wts["ln_g"], wts["ln_b"], a_re, a_im,
      ops["b_in"], ops["c_re"], ops["c_im"], ops["k_mix"], wts["d_skip"], wts["w_glu"], wts["b_glu"],
      wts["sgu_w"], wts["s_row"], wts["s_bias"], wts["w_pa"], wts["w_pb"])
    merged, h_re, h_im = outs[:3]
    return merged.reshape(bn * ln, d), h_re, h_im, (outs[3] if single_position else None)


def _ffn_kernel(x_ref, m_ref, p_ref, wout_ref, bout_ref, g1_ref, b1_ref, wup_ref, bup_ref, wdn_ref, bdn_ref,
                wple_ref, wpg_ref, bpg_ref, g2_ref, b2_ref, x2_ref, *, alpha):
    mix = _dot(m_ref[...], wout_ref[...]) + bout_ref[...]
    x1 = _layer_norm(alpha * x_ref[...] + mix, g1_ref[...], b1_ref[...])
    x1b = x1.astype(BF16)
    hidden = jnp.square(jnp.maximum(_dot(x1b, wup_ref[...]) + bup_ref[...], 0.0))
    ff = _dot(hidden.astype(BF16), wdn_ref[...]) + bdn_ref[...]
    ple = _sigmoid(_dot(x1b, wpg_ref[...]) + bpg_ref[...]) * _dot(p_ref[...].astype(BF16), wple_ref[...])
    x2_ref[...] = _layer_norm(alpha * x1 + ff + ple, g2_ref[...], b2_ref[...])


def _ffn(x2d, merged, p2d, wts, alpha):
    m, d = x2d.shape
    tm = min(FFN_ROW_TILE, m)
    names = ("w_out", "b_out", "ln1_g", "ln1_b", "w_up", "b_up", "w_down", "b_down", "w_ple", "w_pg", "b_pg",
             "ln2_g", "ln2_b")
    return pl.pallas_call(
        functools.partial(_ffn_kernel, alpha=alpha),
        grid=(m // tm,),
        in_specs=[_row_spec(tm, d), _row_spec(tm, d), _row_spec(tm, p2d.shape[1])]
                 + [_resident(wts[k].shape) for k in names],
        out_specs=_row_spec(tm, d),
        out_shape=jax.ShapeDtypeStruct((m, d), F32),
        compiler_params=_params("parallel"),
        name="ffn",
    )(x2d, merged, p2d, *[wts[k] for k in names])


def _decoder_layer(x, p, h0_re, h0_im, wts, ops, alpha):
    bn, ln, d = x.shape
    chunk = wts["sgu_w"].shape[1]
    npos = min(chunk, ln)
    q = math.gcd(npos, S5_CHUNK)
    merged, h_re, h_im, v_b = _mixer(x, h0_re, h0_im, wts, ops, q, npos)
    x2 = _ffn(x.reshape(bn * ln, d), merged, p.reshape(bn * ln, p.shape[-1]), wts, alpha)
    return x2.reshape(bn, ln, d), h_re, h_im, v_b


def kernel(x_prompt, x_sample, state_ssm_re, state_ssm_im, p_prompt, p_sample, w_in, b_in, ssm_lambda_re, ssm_lambda_im, ssm_log_dt, ssm_b_re, ssm_b_im, ssm_c_re, ssm_c_im, ssm_d, w_glu, b_glu, sgu_ln_g, sgu_ln_b, sgu_w, sgu_b, w_branch_a, w_branch_b, w_out, b_out, ln1_g, ln1_b, w_up, b_up, w_down, b_down, w_ple, w_ple_gate, b_ple_gate, ln2_g, ln2_b):
    depth = w_in.shape[0]
    g, p_state = ssm_lambda_re.shape[1:]
    heads, chunk = sgu_b.shape[1:]
    sgu_width = sgu_ln_g.shape[1]
    hd = sgu_width // heads
    alpha = float((2 * depth) ** 0.25)
    row = lambda a: a.reshape(1, -1)

    xp, xs = x_prompt, x_sample
    bp = x_prompt.shape[0]
    bs = x_sample.shape[0]
    zeros_p = jnp.zeros((bp, g * p_state), F32)
    outs = {k: [] for k in ("pre", "pim", "sre", "sim", "sv")}
    for i in range(depth):
        ops = _discretise(ssm_lambda_re[i], ssm_lambda_im[i], ssm_log_dt[i], ssm_b_re[i], ssm_b_im[i],
                          ssm_c_re[i], ssm_c_im[i], S5_CHUNK)
        wts = dict(
            w_in=w_in[i].astype(BF16), b_in=row(b_in[i]), ln_g=row(sgu_ln_g[i]), ln_b=row(sgu_ln_b[i]),
            d_skip=row(ssm_d[i]), w_glu=w_glu[i].astype(BF16), b_glu=row(b_glu[i]),
            sgu_w=sgu_w[i],
            s_row=jnp.repeat(sgu_w[i][:, 0, 0], hd).reshape(1, sgu_width),
            s_bias=jnp.repeat(sgu_b[i].T, hd, axis=1),
            w_pa=w_branch_a[i].astype(BF16), w_pb=w_branch_b[i].astype(BF16),
            w_out=w_out[i].astype(BF16), b_out=row(b_out[i]), ln1_g=row(ln1_g[i]), ln1_b=row(ln1_b[i]),
            w_up=w_up[i].astype(BF16), b_up=row(b_up[i]), w_down=w_down[i].astype(BF16), b_down=row(b_down[i]),
            w_ple=w_ple[i].astype(BF16), w_pg=w_ple_gate[i].astype(BF16), b_pg=row(b_ple_gate[i]),
            ln2_g=row(ln2_g[i]), ln2_b=row(ln2_b[i]),
        )
        xp, hpr, hpi, _ = _decoder_layer(xp, p_prompt[i], zeros_p, zeros_p, wts, ops, alpha)
        xs, hsr, hsi, vs = _decoder_layer(xs, p_sample[i], state_ssm_re[i].reshape(bs, g * p_state),
                                          state_ssm_im[i].reshape(bs, g * p_state), wts, ops, alpha)
        outs["pre"].append(hpr.reshape(bp, g, p_state))
        outs["pim"].append(hpi.reshape(bp, g, p_state))
        outs["sre"].append(hsr.reshape(bs, g, p_state))
        outs["sim"].append(hsi.reshape(bs, g, p_state))
        outs["sv"].append(vs.reshape(bs, x_sample.shape[1], sgu_width))
    return (xp, xs, jnp.stack(outs["pre"]), jnp.stack(outs["pim"]), jnp.stack(outs["sre"]),
            jnp.stack(outs["sim"]), jnp.stack(outs["sv"]))
```
